```python
import math
import jax
import jax.numpy as jnp
from jax import lax
import numpy as np

D_MODEL = 1024
BATCH = 8
SEQ = 8192
DEPTH = 2

GRID_W = 64
CTX_LEN = 256
EPS = 1e-6

SSM_WIDTH = D_MODEL // 2
SSM_GROUP = 16
SSM_GROUPS = SSM_WIDTH // SSM_GROUP
SSM_STATE = 64

DN_HEADS = 4
DN_DK = 128
DN_DV = 128
DN_CONV = 5
DN_CHUNK = 64

AT_HEADS = 8
AT_KV = 2
AT_HD = 64
WINDOW = 128
AT_BLOCK = 128
ROPE_BASE = 10000.0

D_FF = 4 * D_MODEL
N_BRANCH = 3
N_MOD = 6

IN_SPLITS = (SSM_WIDTH, DN_HEADS * DN_DK, DN_HEADS * DN_DK, DN_HEADS * DN_DV, DN_HEADS * DN_DV,
             2 * DN_HEADS, 2 * DN_HEADS, AT_HEADS * AT_HD, AT_KV * AT_HD, AT_KV * AT_HD, N_BRANCH * D_MODEL)
D_IN = sum(IN_SPLITS)

kernel_name = "hybrid_s5_deltanet_swa_dit_block"


def rmsnorm(x, g):
    xf = x.astype(jnp.float32)
    y = xf * lax.rsqrt(jnp.mean(xf * xf, axis=-1, keepdims=True) + EPS)
    return (y * g.astype(jnp.float32)).astype(x.dtype)


def l2norm(x):
    return x * lax.rsqrt(jnp.sum(x * x, axis=-1, keepdims=True) + EPS)


def split_cols(t, sizes):
    idx = np.cumsum(np.array(sizes))[:-1].tolist()
    return jnp.split(t, idx, axis=-1)


def adaln_modulation(cond, w_mod, b_mod):
    m = jax.nn.silu(cond) @ w_mod + b_mod
    return jnp.split(m[..., None, :], N_MOD, axis=-1)


def s5_discretize(lam_re, lam_im, log_dt, b_re, b_im):
    lam_re = lam_re.astype(jnp.float32)
    lam_im = lam_im.astype(jnp.float32)
    b_re = b_re.astype(jnp.float32)
    b_im = b_im.astype(jnp.float32)
    dt = jnp.exp(log_dt.astype(jnp.float32))[:, None]
    mag = jnp.exp(lam_re * dt)
    a_re = mag * jnp.cos(lam_im * dt)
    a_im = mag * jnp.sin(lam_im * dt)
    den = lam_re * lam_re + lam_im * lam_im
    f_re = ((a_re - 1.0) * lam_re + a_im * lam_im) / den
    f_im = (a_im * lam_re - (a_re - 1.0) * lam_im) / den
    bb_re = f_re[..., None] * b_re - f_im[..., None] * b_im
    bb_im = f_re[..., None] * b_im + f_im[..., None] * b_re
    return a_re, a_im, bb_re, bb_im


def complex_affine_combine(e1, e2):
    a1r, a1i, b1r, b1i = e1
    a2r, a2i, b2r, b2i = e2
    return (a2r * a1r - a2i * a1i,
            a2r * a1i + a2i * a1r,
            a2r * b1r - a2i * b1i + b2r,
            a2r * b1i + a2i * b1r + b2i)


def s5_scan(a_re, a_im, bb_re, bb_im, u, s0_re, s0_im, reverse):
    bu_re = jnp.einsum('blgh,gph->blgp', u, bb_re)
    bu_im = jnp.einsum('blgh,gph->blgp', u, bb_im)
    first = -1 if reverse else 0
    bu_re = bu_re.at[:, first].add(a_re * s0_re - a_im * s0_im)
    bu_im = bu_im.at[:, first].add(a_re * s0_im + a_im * s0_re)
    L = u.shape[1]
    ar = jnp.broadcast_to(a_re, (1, L) + a_re.shape)
    ai = jnp.broadcast_to(a_im, (1, L) + a_im.shape)
    _, _, s_re, s_im = lax.associative_scan(complex_affine_combine, (ar, ai, bu_re, bu_im),
                                            reverse=reverse, axis=1)
    return s_re, s_im


def s5_readout(s_re, s_im, c_re, c_im):
    return jnp.einsum('blgp,ghp->blgh', s_re, c_re) - jnp.einsum('blgp,ghp->blgh', s_im, c_im)


def s5_glu(y, w_glu):
    z = jax.nn.gelu(y)
    return z * jax.nn.sigmoid(z @ w_glu.astype(jnp.float32))


def s5_branch(u, u_c, lam_re, lam_im, log_dt, b_re, b_im, c_re, c_im, d_skip, w_glu, with_ctx_out):
    Bn, L, _ = u.shape
    Lc = u_c.shape[1]
    ug = u.astype(jnp.float32).reshape(Bn, L, SSM_GROUPS, SSM_GROUP)
    ucg = u_c.astype(jnp.float32).reshape(Bn, Lc, SSM_GROUPS, SSM_GROUP)
    d = d_skip.astype(jnp.float32).reshape(SSM_GROUPS, SSM_GROUP)
    y = ug * d
    y_c = ucg * d if with_ctx_out else None
    zero = jnp.zeros((Bn, SSM_GROUPS, SSM_STATE), jnp.float32)
    for direction in range(2):
        rev = direction == 1
        a_re, a_im, bb_re, bb_im = s5_discretize(lam_re[direction], lam_im[direction], log_dt[direction],
                                                 b_re[direction], b_im[direction])
        cr = c_re[direction].astype(jnp.float32)
        ci = c_im[direction].astype(jnp.float32)
        sc_re, sc_im = s5_scan(a_re, a_im, bb_re, bb_im, ucg, zero, zero, rev)
        end = 0 if rev else -1
        s_re, s_im = s5_scan(a_re, a_im, bb_re, bb_im, ug, sc_re[:, end], sc_im[:, end], rev)
        y = y + s5_readout(s_re, s_im, cr, ci)
        if with_ctx_out:
            y_c = y_c + s5_readout(sc_re, sc_im, cr, ci)
    out = s5_glu(y.reshape(Bn, L, SSM_WIDTH), w_glu).astype(u.dtype)
    out_c = s5_glu(y_c.reshape(Bn, Lc, SSM_WIDTH), w_glu).astype(u.dtype) if with_ctx_out else None
    return out, out_c


def dn_prepare(q, k, v, b, a, conv_w, a_log, dt_bias):
    Bn, L, _ = q.shape
    qkv = jnp.concatenate([q, k, v], axis=-1)
    qkv = lax.conv_general_dilated(qkv, conv_w[:, None, :].astype(qkv.dtype), window_strides=(1,),
                                   padding=[(DN_CONV // 2, DN_CONV // 2)],
                                   dimension_numbers=('NWC', 'WIO', 'NWC'),
                                   feature_group_count=qkv.shape[-1])
    qkv = jax.nn.silu(qkv.astype(jnp.float32))
    qf, kf, vf = jnp.split(qkv, [DN_HEADS * DN_DK, 2 * DN_HEADS * DN_DK], axis=-1)
    qf = l2norm(qf.reshape(Bn, L, DN_HEADS, DN_DK)) * (DN_DK ** -0.5)
    kf = l2norm(kf.reshape(Bn, L, DN_HEADS, DN_DK))
    vf = vf.reshape(Bn, L, DN_HEADS, DN_DV)
    beta = jax.nn.sigmoid(b.astype(jnp.float32)).reshape(Bn, L, 2, DN_HEADS)
    g = -jnp.exp(a_log.astype(jnp.float32)) * jax.nn.softplus(
        a.astype(jnp.float32).reshape(Bn, L, 2, DN_HEADS) + dt_bias.astype(jnp.float32))
    return qf, kf, vf, beta, g


def gated_delta_chunked(q, k, v, beta, g, s0):
    Bn, L, H, dk = q.shape
    dv = v.shape[-1]
    n = L // DN_CHUNK
    C = DN_CHUNK

    def chunks(t):
        return t.reshape(Bn, n, C, H, t.shape[-1]).transpose(1, 0, 3, 2, 4)

    qc, kc, vc = chunks(q), chunks(k), chunks(v)
    bc = beta.reshape(Bn, n, C, H).transpose(1, 0, 3, 2)
    gc = jnp.cumsum(g.reshape(Bn, n, C, H).transpose(1, 0, 3, 2), axis=-1)
    tri = jnp.tril(jnp.ones((C, C), dtype=bool))
    strict = jnp.tril(jnp.ones((C, C), dtype=bool), -1)
    decay = jnp.exp(jnp.where(tri, gc[..., :, None] - gc[..., None, :], -jnp.inf))
    kb = kc * bc[..., None]
    eye = jnp.eye(C, dtype=jnp.float32)
    a_mat = eye + jnp.where(strict, jnp.einsum('nbhik,nbhjk->nbhij', kb, kc) * decay, 0.0)
    rhs = jnp.concatenate([vc * bc[..., None], kb * jnp.exp(gc)[..., None]], axis=-1)
    sol = lax.linalg.triangular_solve(a_mat, rhs, left_side=True, lower=True, unit_diagonal=True)
    u_c, w_c = sol[..., :dv], sol[..., dv:]
    qk = jnp.einsum('nbhik,nbhjk->nbhij', qc, kc) * decay

    def step(S, inp):
        q_i, k_i, u_i, w_i, qk_i, g_i = inp
        g_end = g_i[..., -1]
        v_new = u_i - jnp.einsum('bhck,bhkv->bhcv', w_i, S)
        o = (jnp.einsum('bhck,bhkv->bhcv', q_i * jnp.exp(g_i)[..., None], S)
             + jnp.einsum('bhcs,bhsv->bhcv', qk_i, v_new))
        k_dec = k_i * jnp.exp(g_end[..., None] - g_i)[..., None]
        S = S * jnp.exp(g_end)[..., None, None] + jnp.einsum('bhck,bhcv->bhkv', k_dec, v_new)
        return S, o

    S, o = lax.scan(step, s0, (qc, kc, u_c, w_c, qk, gc))
    o = o.transpose(1, 0, 3, 2, 4).reshape(Bn, L, H, dv)
    return o, S


def orient(t, reverse):
    return jnp.flip(t, axis=1) if reverse else t


def dn_output(o, z, norm_g):
    Bn, L = o.shape[:2]
    zf = z.astype(jnp.float32).reshape(Bn, L, DN_HEADS, DN_DV)
    y = rmsnorm(o, norm_g) * jax.nn.silu(zf)
    return y.reshape(Bn, L, DN_HEADS * DN_DV).astype(z.dtype)


def gated_deltanet_branch(q, k, v, z, b, a, q_c, k_c, v_c, z_c, b_c, a_c,
                          conv_w, a_log, dt_bias, norm_g, with_ctx_out):
    Bn = q.shape[0]
    lq, lk, lv, lbeta, lg = dn_prepare(q, k, v, b, a, conv_w, a_log, dt_bias)
    cq, ck, cv, cbeta, cg = dn_prepare(q_c, k_c, v_c, b_c, a_c, conv_w, a_log, dt_bias)
    zero = jnp.zeros((Bn, DN_HEADS, DN_DK, DN_DV), jnp.float32)
    o = jnp.zeros(lv.shape, jnp.float32)
    o_c = jnp.zeros(cv.shape, jnp.float32)
    for direction in range(2):
        rev = direction == 1
        oc_d, s_ctx = gated_delta_chunked(orient(cq, rev), orient(ck, rev), orient(cv, rev),
                                          orient(cbeta[:, :, direction], rev),
                                          orient(cg[:, :, direction], rev), zero)
        ol_d, _ = gated_delta_chunked(orient(lq, rev), orient(lk, rev), orient(lv, rev),
                                      orient(lbeta[:, :, direction], rev),
                                      orient(lg[:, :, direction], rev), s_ctx)
        o = o + orient(ol_d, rev)
        if with_ctx_out:
            o_c = o_c + orient(oc_d, rev)
    out = dn_output(o, z, norm_g)
    out_c = dn_output(o_c, z_c, norm_g) if with_ctx_out else None
    return out, out_c


def rope_1d(x, pos):
    n = x.shape[-1] // 2
    inv_freq = ROPE_BASE ** (-jnp.arange(n, dtype=jnp.float32) / n)
    ang = pos.astype(jnp.float32)[:, None] * inv_freq[None, :]
    cos = jnp.cos(ang)[None, :, None, :]
    sin = jnp.sin(ang)[None, :, None, :]
    xf = x.astype(jnp.float32)
    x1, x2 = xf[..., :n], xf[..., n:]
    return jnp.concatenate([x1 * cos - x2 * sin, x2 * cos + x1 * sin], axis=-1).astype(x.dtype)


def axial_rope(x, rows, cols):
    half = x.shape[-1] // 2
    return jnp.concatenate([rope_1d(x[..., :half], rows), rope_1d(x[..., half:], cols)], axis=-1)


def banded_window_attention(q, k, v, k_ctx, v_ctx, sink):
    Bn, L, H, hd = q.shape
    KV = k.shape[2]
    G = H // KV
    Lc = k_ctx.shape[1]
    nb = L // AT_BLOCK
    W3 = 3 * AT_BLOCK
    scale = hd ** -0.5
    qb = q.reshape(Bn, nb, AT_BLOCK, KV, G, hd)

    def band(t):
        tp = jnp.pad(t, ((0, 0), (AT_BLOCK, AT_BLOCK), (0, 0), (0, 0)))
        tp = tp.reshape(Bn, nb + 2, AT_BLOCK, KV, hd)
        return jnp.concatenate([tp[:, :-2], tp[:, 1:-1], tp[:, 2:]], axis=2)

    k_band, v_band = band(k), band(v)
    s_loc = jnp.einsum('bnqhgd,bnkhd->bnhgqk', qb, k_band, preferred_element_type=jnp.float32) * scale
    q_pos = jnp.arange(nb)[:, None] * AT_BLOCK + jnp.arange(AT_BLOCK)[None, :]
    k_pos = (jnp.arange(nb)[:, None] - 1) * AT_BLOCK + jnp.arange(W3)[None, :]
    kp = k_pos[:, None, :]
    valid = (jnp.abs(q_pos[:, :, None] - kp) <= WINDOW) & (kp >= 0) & (kp < L)
    s_loc = jnp.where(valid[None, :, None, None], s_loc, -jnp.inf)
    s_ctx = jnp.einsum('bnqhgd,bkhd->bnhgqk', qb, k_ctx, preferred_element_type=jnp.float32) * scale
    s_sink = jnp.broadcast_to(sink.astype(jnp.float32).reshape(KV, G)[None, None, :, :, None, None],
                              s_loc.shape[:-1] + (1,))
    p = jax.nn.softmax(jnp.concatenate([s_loc, s_ctx, s_sink], axis=-1), axis=-1).astype(v.dtype)
    o = (jnp.einsum('bnhgqk,bnkhd->bnqhgd', p[..., :W3], v_band)
         + jnp.einsum('bnhgqk,bkhd->bnqhgd', p[..., W3:W3 + Lc], v_ctx))
    return o.reshape(Bn, L, H * hd)


def context_attention(q, k, v, sink):
    Bn, Lc, H, hd = q.shape
    KV = k.shape[2]
    G = H // KV
    qg = q.reshape(Bn, Lc, KV, G, hd)
    s = jnp.einsum('bqhgd,bkhd->bhgqk', qg, k, preferred_element_type=jnp.float32) * (hd ** -0.5)
    s_sink = jnp.broadcast_to(sink.astype(jnp.float32).reshape(KV, G)[None, :, :, None, None],
                              s.shape[:-1] + (1,))
    p = jax.nn.softmax(jnp.concatenate([s, s_sink], axis=-1), axis=-1)[..., :Lc].astype(v.dtype)
    o = jnp.einsum('bhgqk,bkhd->bqhgd', p, v)
    return o.reshape(Bn, Lc, H * hd)


def window_attention_branch(q, k, v, q_c, k_c, v_c, sink, rows, cols, with_ctx_out):
    Bn, L, _ = q.shape
    Lc = k_c.shape[1]
    qh = axial_rope(q.reshape(Bn, L, AT_HEADS, AT_HD), rows, cols)
    kh = axial_rope(k.reshape(Bn, L, AT_KV, AT_HD), rows, cols)
    vh = v.reshape(Bn, L, AT_KV, AT_HD)
    kch = k_c.reshape(Bn, Lc, AT_KV, AT_HD)
    vch = v_c.reshape(Bn, Lc, AT_KV, AT_HD)
    out = banded_window_attention(qh, kh, vh, kch, vch, sink)
    out_c = context_attention(q_c.reshape(Bn, Lc, AT_HEADS, AT_HD), kch, vch, sink) if with_ctx_out else None
    return out, out_c


def merge_branches(ya, yb, yc, gate_logits, w_ba, w_bb, w_bc, w_out):
    ga, gb, gc = jnp.split(jax.nn.sigmoid(gate_logits), N_BRANCH, axis=-1)
    m = ga * (ya @ w_ba) + gb * (yb @ w_bb) + gc * (yc @ w_bc)
    return m @ w_out


def sq_relu_mlp(h, w1, w2):
    return jnp.square(jax.nn.relu(h @ w1)) @ w2


def hybrid_mixer(h, h_c, w_in, lam_re, lam_im, log_dt, b_re, b_im, c_re, c_im, d_skip, w_glu,
                 conv_w, a_log, dt_bias, dn_norm_g, sink, w_ba, w_bb, w_bc, w_out,
                 rows, cols, with_ctx_out):
    (u, dq, dk, dv, dz, db, da, aq, ak, av, gates) = split_cols(h @ w_in, IN_SPLITS)
    (u_c, dq_c, dk_c, dv_c, dz_c, db_c, da_c, aq_c, ak_c, av_c, gates_c) = split_cols(h_c @ w_in, IN_SPLITS)
    ya, ya_c = s5_branch(u, u_c, lam_re, lam_im, log_dt, b_re, b_im, c_re, c_im, d_skip, w_glu, with_ctx_out)
    yb, yb_c = gated_deltanet_branch(dq, dk, dv, dz, db, da, dq_c, dk_c, dv_c, dz_c, db_c, da_c,
                                     conv_w, a_log, dt_bias, dn_norm_g, with_ctx_out)
    yc, yc_c = window_attention_branch(aq, ak, av, aq_c, ak_c, av_c, sink, rows, cols, with_ctx_out)
    out = merge_branches(ya, yb, yc, gates, w_ba, w_bb, w_bc, w_out)
    out_c = merge_branches(ya_c, yb_c, yc_c, gates_c, w_ba, w_bb, w_bc, w_out) if with_ctx_out else None
    return out, out_c


def setup_inputs(seed: int = 0) -> dict:
    key = jax.random.key(seed)
    keys = jax.random.split(key, 40)
    counter = iter(range(40))

    def nk():
        return keys[next(counter)]

    def nrm(shape, scale):
        return jax.random.normal(nk(), shape, jnp.float32) * scale

    def unif(shape, lo, hi):
        return jax.random.uniform(nk(), shape, jnp.float32, lo, hi)

    G, P, H = SSM_GROUPS, SSM_STATE, SSM_GROUP
    dn_dt = jnp.exp(unif((DEPTH, 2, DN_HEADS), math.log(1e-3), math.log(1e-1)))
    return {
        'x': nrm((BATCH, SEQ, D_MODEL), 1.0),
        'c': nrm((BATCH, D_MODEL), 1.0),
        'ctx': nrm((BATCH, CTX_LEN, D_MODEL), 1.0),
        'c_ctx': nrm((D_MODEL,), 1.0),
        'norm1_g': 1.0 + nrm((DEPTH, D_MODEL), 0.02),
        'norm2_g': 1.0 + nrm((DEPTH, D_MODEL), 0.02),
        'w_mod': nrm((DEPTH, D_MODEL, N_MOD * D_MODEL), 0.5 * D_MODEL ** -0.5),
        'b_mod': nrm((DEPTH, N_MOD * D_MODEL), 0.02),
        'w_in': nrm((DEPTH, D_MODEL, D_IN), D_MODEL ** -0.5),
        'ssm_lam_re': -0.5 + nrm((DEPTH, 2, G, P), 0.01),
        'ssm_lam_im': math.pi * jnp.arange(P, dtype=jnp.float32) + nrm((DEPTH, 2, G, P), 0.01),
        'ssm_log_dt': unif((DEPTH, 2, G), math.log(1e-3), math.log(1e-1)),
        'ssm_b_re': nrm((DEPTH, 2, G, P, H), (2 * H) ** -0.5),
        'ssm_b_im': nrm((DEPTH, 2, G, P, H), (2 * H) ** -0.5),
        'ssm_c_re': nrm((DEPTH, 2, G, H, P), (2 * P) ** -0.5),
        'ssm_c_im': nrm((DEPTH, 2, G, H, P), (2 * P) ** -0.5),
        'ssm_d': nrm((DEPTH, SSM_WIDTH), 1.0),
        'ssm_w_glu': nrm((DEPTH, SSM_WIDTH, SSM_WIDTH), SSM_WIDTH ** -0.5),
        'dn_conv_w': nrm((DEPTH, DN_CONV, 2 * DN_HEADS * DN_DK + DN_HEADS * DN_DV), DN_CONV ** -0.5),
        'dn_a_log': jnp.log(unif((DEPTH, 2, DN_HEADS), 1.0, 16.0)),
        'dn_dt_bias': dn_dt + jnp.log(-jnp.expm1(-dn_dt)),
        'dn_norm_g': 1.0 + nrm((DEPTH, DN_DV), 0.02),
        'attn_sink': nrm((DEPTH, AT_HEADS), 0.5),
        'w_branch_a': nrm((DEPTH, SSM_WIDTH, D_MODEL), SSM_WIDTH ** -0.5),
        'w_branch_b': nrm((DEPTH, DN_HEADS * DN_DV, D_MODEL), (DN_HEADS * DN_DV) ** -0.5),
        'w_branch_c': nrm((DEPTH, AT_HEADS * AT_HD, D_MODEL), (AT_HEADS * AT_HD) ** -0.5),
        'w_out': nrm((DEPTH, D_MODEL, D_MODEL), D_MODEL ** -0.5),
        'w_ff1': nrm((DEPTH, D_MODEL, D_FF), D_MODEL ** -0.5),
        'w_ff2': nrm((DEPTH, D_FF, D_MODEL), D_FF ** -0.5),
        'final_norm_g': 1.0 + nrm((D_MODEL,), 0.02),
    }


def reference(x, c, ctx, c_ctx, norm1_g, norm2_g, w_mod, b_mod, w_in,
              ssm_lam_re, ssm_lam_im, ssm_log_dt, ssm_b_re, ssm_b_im, ssm_c_re, ssm_c_im, ssm_d, ssm_w_glu,
              dn_conv_w, dn_a_log, dn_dt_bias, dn_norm_g, attn_sink,
              w_branch_a, w_branch_b, w_branch_c, w_out, w_ff1, w_ff2, final_norm_g):
    L = x.shape[1]
    ROWS = L // GRID_W
    rows = jnp.repeat(jnp.arange(ROWS, dtype=jnp.int32), GRID_W)
    cols = jnp.tile(jnp.arange(GRID_W, dtype=jnp.int32), ROWS)
    for layer in range(DEPTH):
        with_ctx_out = layer < DEPTH - 1
        sh1, sc1, g1, sh2, sc2, g2 = adaln_modulation(c, w_mod[layer], b_mod[layer])
        csh1, csc1, cg1, csh2, csc2, cg2 = adaln_modulation(c_ctx, w_mod[layer], b_mod[layer])
        h = rmsnorm(x, norm1_g[layer]) * (1.0 + sc1) + sh1
        h_c = rmsnorm(ctx, norm1_g[layer]) * (1.0 + csc1) + csh1
        mix, mix_c = hybrid_mixer(h, h_c, w_in[layer],
                                  ssm_lam_re[layer], ssm_lam_im[layer], ssm_log_dt[layer],
                                  ssm_b_re[layer], ssm_b_im[layer], ssm_c_re[layer], ssm_c_im[layer],
                                  ssm_d[layer], ssm_w_glu[layer],
                                  dn_conv_w[layer], dn_a_log[layer], dn_dt_bias[layer], dn_norm_g[layer],
                                  attn_sink[layer],
                                  w_branch_a[layer], w_branch_b[layer], w_branch_c[layer], w_out[layer],
                                  rows, cols, with_ctx_out)
        x = x + g1 * mix
        x = x + g2 * sq_relu_mlp(rmsnorm(x, norm2_g[layer]) * (1.0 + sc2) + sh2, w_ff1[layer], w_ff2[layer])
        if with_ctx_out:
            ctx = ctx + cg1 * mix_c
            ctx = ctx + cg2 * sq_relu_mlp(rmsnorm(ctx, norm2_g[layer]) * (1.0 + csc2) + csh2,
                                          w_ff1[layer], w_ff2[layer])
    return rmsnorm(x, final_norm_g)
```

```python
import functools
import math

import jax
import jax.numpy as jnp
from jax import lax
from jax.experimental import pallas as pl
from jax.experimental.pallas import tpu as pltpu

f32 = jnp.float32
bf16 = jnp.bfloat16
HI = lax.Precision.HIGHEST

D_MODEL = 1024
GRID_W = 64
EPS = 1e-6
SSM_WIDTH = D_MODEL // 2
SSM_GROUP = 16
SSM_GROUPS = SSM_WIDTH // SSM_GROUP
SSM_STATE = 64
DN_HEADS = 4
DN_DK = 128
DN_DV = 128
DN_CONV = 5
DN_CHUNK = 64
AT_HEADS = 8
AT_KV = 2
AT_HD = 64
WINDOW = 128
AT_BLOCK = 128
ROPE_BASE = 10000.0
D_FF = 4 * D_MODEL
N_MOD = 6
DN_W = DN_HEADS * DN_DK
AT_QW = AT_HEADS * AT_HD
AT_KW = AT_KV * AT_HD

SUBLANES = 8
LANES = 128
TOK_TILE = 256
S5_STEPS = 32
S5_COLBLK = 128
S5_NBLK = SSM_WIDTH // S5_COLBLK
S5_SBLK = (S5_COLBLK // SSM_GROUP) * SSM_STATE
VMEM_LIMIT = 56 * 1024 * 1024


def _cparams(sem):
    return pltpu.CompilerParams(dimension_semantics=sem, vmem_limit_bytes=VMEM_LIMIT)


def _const_spec(shape):
    nd = len(shape)
    return pl.BlockSpec(shape, lambda *_: (0,) * nd, pipeline_mode=pl.Buffered(1))


def _dot(a, b):
    return jnp.dot(a.astype(bf16), b.astype(bf16), preferred_element_type=f32)


def _dot_nt(a, b):
    return lax.dot_general(a.astype(bf16), b.astype(bf16), (((1,), (1,)), ((), ())),
                           preferred_element_type=f32)


def _dot_tn(a, b):
    return lax.dot_general(a.astype(bf16), b.astype(bf16), (((0,), (0,)), ((), ())),
                           preferred_element_type=f32)


def _modnorm(x, g, scale, shift):
    y = x * lax.rsqrt(jnp.mean(x * x, axis=-1, keepdims=True) + EPS)
    return (y * g) * (1.0 + scale) + shift


def _silu(x):
    return x * jax.nn.sigmoid(x)


def _mod_body(c_ref, w_ref, b_ref, o_ref):
    s = _silu(c_ref[...])
    o_ref[0] = jnp.dot(s, w_ref[0], precision=HI, preferred_element_type=f32) + b_ref[0]


def _modulation(cond, w_mod, b_mod):
    depth = w_mod.shape[0]
    nblk = (N_MOD * D_MODEL) // D_MODEL
    return pl.pallas_call(
        _mod_body,
        out_shape=jax.ShapeDtypeStruct((depth, 16, N_MOD * D_MODEL), f32),
        grid=(depth, nblk),
        in_specs=[pl.BlockSpec((16, D_MODEL), lambda l, j: (0, 0)),
                  pl.BlockSpec((1, D_MODEL, D_MODEL), lambda l, j: (l, 0, j)),
                  pl.BlockSpec((1, 1, D_MODEL), lambda l, j: (l, 0, j))],
        out_specs=pl.BlockSpec((1, 16, D_MODEL), lambda l, j: (l, 0, j)),
        compiler_params=_cparams(("arbitrary", "arbitrary")),
        name="adaln_mod",
    )(cond, w_mod, b_mod.reshape(depth, 1, N_MOD * D_MODEL))


def _rope(x, cos, sin):
    n = x.shape[-1]
    lane = lax.broadcasted_iota(jnp.int32, x.shape, 1)
    partner = jnp.where((lane % 32) < 16, pltpu.roll(x, n - 16, 1), pltpu.roll(x, 16, 1))
    return x * cos + partner * sin


def _inproj_body(x_ref, mod_ref, g_ref, wm_ref, wba_ref, cos_ref, sin_ref,
                 u_ref, qkv_ref, z_ref, ba_ref, aq_ref, ak_ref, av_ref):
    h = _modnorm(x_ref[0], g_ref[...], mod_ref[0, 1:2, :], mod_ref[0, 0:1, :]).astype(bf16)

    def proj(lo, hi):
        return jnp.dot(h, wm_ref[:, lo:hi], preferred_element_type=f32)

    o = 0
    u_ref[...] = proj(o, o + SSM_WIDTH)
    o += SSM_WIDTH
    qkv_ref[0] = proj(o, o + 3 * DN_W)
    o += 3 * DN_W
    z_ref[0] = proj(o, o + DN_W)
    o += DN_W
    cos = cos_ref[...]
    sin = sin_ref[...]
    reps = AT_QW // LANES
    aq_ref[0] = _rope(proj(o, o + AT_QW), jnp.concatenate([cos] * reps, axis=1),
                      jnp.concatenate([sin] * reps, axis=1))
    o += AT_QW
    ak_ref[0] = _rope(proj(o, o + AT_KW), cos, sin)
    o += AT_KW
    av_ref[0] = proj(o, o + AT_KW)
    ba_ref[0] = jnp.dot(h, wba_ref[...], preferred_element_type=f32)


def _mod_index(n_ctx_tiles, ctx_row):
    return lambda b, t: (jnp.where(t < n_ctx_tiles, ctx_row, b), 0, 0)


def _inproj(xcat, mods, g1, w_main, w_ba, cos_t, sin_t, n_ctx_tiles):
    B, Lt, D = xcat.shape
    nt = Lt // TOK_TILE
    wm = w_main.shape[1]
    tok = lambda w: pl.BlockSpec((1, TOK_TILE, w), lambda b, t: (b, t, 0))
    out_shape = (
        jax.ShapeDtypeStruct((Lt, B * SSM_WIDTH), f32),
        jax.ShapeDtypeStruct((B, Lt, 3 * DN_W), f32),
        jax.ShapeDtypeStruct((B, Lt, DN_W), f32),
        jax.ShapeDtypeStruct((B, Lt, LANES), f32),
        jax.ShapeDtypeStruct((B, Lt, AT_QW), f32),
        jax.ShapeDtypeStruct((B, Lt, AT_KW), f32),
        jax.ShapeDtypeStruct((B, Lt, AT_KW), f32),
    )
    return pl.pallas_call(
        _inproj_body,
        out_shape=out_shape,
        grid=(B, nt),
        in_specs=[tok(D),
                  pl.BlockSpec((1, N_MOD, D), _mod_index(n_ctx_tiles, B)),
                  _const_spec((1, D)),
                  _const_spec((D, wm)),
                  _const_spec((D, LANES)),
                  pl.BlockSpec((TOK_TILE, LANES), lambda b, t: (t, 0)),
                  pl.BlockSpec((TOK_TILE, LANES), lambda b, t: (t, 0))],
        out_specs=(pl.BlockSpec((TOK_TILE, SSM_WIDTH), lambda b, t: (t, b)),
                   tok(3 * DN_W), tok(DN_W), tok(LANES), tok(AT_QW), tok(AT_KW), tok(AT_KW)),
        compiler_params=_cparams(("arbitrary", "arbitrary")),
        name="in_proj",
    )(xcat, mods, g1, w_main, w_ba, cos_t, sin_t)


def _s5disc_body(lre_ref, lim_ref, ldt_ref, bre_ref, bim_ref, are_ref, aim_ref, bbre_ref, bbim_ref):
    lr = lre_ref[0]
    li = lim_ref[0]
    dt = jnp.exp(ldt_ref[0])
    mag = jnp.exp(lr * dt)
    a_re = mag * jnp.cos(li * dt)
    a_im = mag * jnp.sin(li * dt)
    den = lr * lr + li * li
    f_re = ((a_re - 1.0) * lr + a_im * li) / den
    f_im = (a_im * lr - (a_re - 1.0) * li) / den
    are_ref[0] = a_re
    aim_ref[0] = a_im
    b_re = bre_ref[0]
    b_im = bim_ref[0]
    bbre_ref[0] = f_re * b_re - f_im * b_im
    bbim_ref[0] = f_re * b_im + f_im * b_re


def _s5_discretize(lam_re, lam_im, log_dt, b_re, b_im):
    n = 2 * SSM_GROUPS
    P, H = SSM_STATE, SSM_GROUP
    vec = pl.BlockSpec((1, 1, P), lambda i: (i, 0, 0))
    mat = pl.BlockSpec((1, H, P), lambda i: (i, 0, 0))
    return pl.pallas_call(
        _s5disc_body,
        out_shape=(jax.ShapeDtypeStruct((n, 1, P), f32), jax.ShapeDtypeStruct((n, 1, P), f32),
                   jax.ShapeDtypeStruct((n, H, P), f32), jax.ShapeDtypeStruct((n, H, P), f32)),
        grid=(n,),
        in_specs=[vec, vec, pl.BlockSpec((1, 1, 1), lambda i: (i, 0, 0)), mat, mat],
        out_specs=(vec, vec, mat, mat),
        compiler_params=_cparams(("arbitrary",)),
        name="s5_discretize",
    )(lam_re.reshape(n, 1, P), lam_im.reshape(n, 1, P), log_dt.reshape(n, 1, 1),
      jnp.swapaxes(b_re, -1, -2).reshape(n, H, P), jnp.swapaxes(b_im, -1, -2).reshape(n, H, P))


def _s5_matrices(a_re, a_im, bbt_re, bbt_im, c_re, c_im):
    gb = S5_COLBLK // SSM_GROUP
    P, H = SSM_STATE, SSM_GROUP
    eye = jnp.eye(gb, dtype=f32)

    def bblk(t):
        t = t.reshape(2, S5_NBLK, gb, H, P)
        return jnp.einsum('dcghp,gk->dcghkp', t, eye).reshape(2, S5_NBLK, gb * H, gb * P)

    def cblk(t):
        t = t.reshape(2, S5_NBLK, gb, H, P)
        return jnp.einsum('dcghp,gk->dcgpkh', t, eye).reshape(2, S5_NBLK, gb * P, gb * H)

    bmat = jnp.concatenate([bblk(bbt_re), bblk(bbt_im)], axis=-1).astype(bf16)
    cmat = jnp.concatenate([cblk(c_re.astype(f32)), -cblk(c_im.astype(f32))], axis=-2).astype(bf16)
    avec = jnp.stack([a_re.reshape(2, S5_NBLK, gb * P), a_im.reshape(2, S5_NBLK, gb * P)], axis=2)
    return bmat, cmat, avec


def _s5_scan_body(*refs, rev):
    if rev:
        u_ref, bm_ref, cm_ref, a_ref, yf_ref, d_ref, y_ref, bu_scr, st_scr = refs
    else:
        u_ref, bm_ref, cm_ref, a_ref, y_ref, bu_scr, st_scr = refs
    B = SUBLANES
    nsb = S5_SBLK

    @pl.when(pl.program_id(0) == 0)
    def _():
        st_scr[...] = jnp.zeros_like(st_scr)

    for cb in range(S5_NBLK):
        cols = slice(cb * S5_COLBLK, (cb + 1) * S5_COLBLK)
        bu_scr[...] = jnp.dot(u_ref[:, cols].astype(bf16), bm_ref[0, cb], preferred_element_type=f32)
        are = jnp.broadcast_to(a_ref[0, cb, 0:1, :], (B, nsb))
        aim = jnp.broadcast_to(a_ref[0, cb, 1:2, :], (B, nsb))

        def step(i, carry, are=are, aim=aim):
            sre, sim = carry
            t = (S5_STEPS - 1 - i) if rev else i
            r0 = pl.multiple_of(t * B, B)
            nre = are * sre - aim * sim + bu_scr[pl.ds(r0, B), 0:nsb]
            nim = are * sim + aim * sre + bu_scr[pl.ds(r0, B), nsb:2 * nsb]
            bu_scr[pl.ds(r0, B), 0:nsb] = nre
            bu_scr[pl.ds(r0, B), nsb:2 * nsb] = nim
            return nre, nim

        sre, sim = lax.fori_loop(0, S5_STEPS, step, (st_scr[cb, :, 0:nsb], st_scr[cb, :, nsb:2 * nsb]),
                                 unroll=4)
        st_scr[cb, :, 0:nsb] = sre
        st_scr[cb, :, nsb:2 * nsb] = sim
        y = jnp.dot(bu_scr[...].astype(bf16), cm_ref[0, cb], preferred_element_type=f32)
        if rev:
            y = y + yf_ref[:, cols] + d_ref[:, cols] * u_ref[:, cols]
        y_ref[:, cols] = y


def _s5_tile_index(n_ctx_tiles, n_tiles, rev):
    if not rev:
        return lambda i: (i, 0)
    return lambda i: (jnp.where(i < n_ctx_tiles, n_ctx_tiles - 1 - i, n_tiles - 1 - (i - n_ctx_tiles)), 0)


def _s5_scan(u_rows, bmat, cmat, avec, n_ctx_steps, rev, yf=None, d_skip=None):
    n_rows = u_rows.shape[0]
    rows = S5_STEPS * SUBLANES
    n_tiles = n_rows // rows
    n_ctx_tiles = n_ctx_steps // S5_STEPS
    idx = _s5_tile_index(n_ctx_tiles, n_tiles, rev)
    d = 1 if rev else 0
    tile = pl.BlockSpec((rows, SSM_WIDTH), idx)
    in_specs = [tile,
                pl.BlockSpec((1, S5_NBLK, S5_COLBLK, 2 * S5_SBLK), lambda i: (d, 0, 0, 0), pipeline_mode=pl.Buffered(1)),
                pl.BlockSpec((1, S5_NBLK, 2 * S5_SBLK, S5_COLBLK), lambda i: (d, 0, 0, 0), pipeline_mode=pl.Buffered(1)),
                pl.BlockSpec((1, S5_NBLK, 2, S5_SBLK), lambda i: (d, 0, 0, 0), pipeline_mode=pl.Buffered(1))]
    args = [u_rows, bmat, cmat, avec]
    if rev:
        in_specs += [tile, _const_spec((1, SSM_WIDTH))]
        args += [yf, d_skip]
    return pl.pallas_call(
        functools.partial(_s5_scan_body, rev=rev),
        out_shape=jax.ShapeDtypeStruct((n_rows, SSM_WIDTH), f32),
        grid=(n_tiles,),
        in_specs=in_specs,
        out_specs=tile,
        scratch_shapes=[pltpu.VMEM((rows, 2 * S5_SBLK), f32),
                        pltpu.VMEM((S5_NBLK, SUBLANES, 2 * S5_SBLK), f32)],
        compiler_params=_cparams(("arbitrary",)),
        name="s5_scan_bwd" if rev else "s5_scan_fwd",
    )(*args)


def _dn_prep_body(qkv_ref, prev_ref, next_ref, ba_ref, cw_ref, al_ref, dtb_ref,
                  q_ref, k_ref, v_ref, gb_ref, gbt_ref, ext_scr, *, n_ctx_tiles, n_tiles):
    T = TOK_TILE
    t = pl.program_id(1)
    prev_ok = jnp.logical_and(t != 0, t != n_ctx_tiles)
    next_ok = jnp.logical_and(t != n_ctx_tiles - 1, t != n_tiles - 1)
    ext_scr[0:SUBLANES, :] = jnp.where(prev_ok, prev_ref[0], 0.0)
    ext_scr[SUBLANES:SUBLANES + T, :] = qkv_ref[0]
    ext_scr[SUBLANES + T:2 * SUBLANES + T, :] = jnp.where(next_ok, next_ref[0], 0.0)
    half = DN_CONV // 2
    acc = cw_ref[0:1, :] * ext_scr[pl.ds(SUBLANES - half, T), :]
    for kk in range(1, DN_CONV):
        acc = acc + cw_ref[kk:kk + 1, :] * ext_scr[pl.ds(SUBLANES - half + kk, T), :]
    act = _silu(acc)
    for hh in range(DN_HEADS):
        qs = act[:, hh * DN_DK:(hh + 1) * DN_DK]
        ks = act[:, DN_W + hh * DN_DK:DN_W + (hh + 1) * DN_DK]
        q_ref[0, :, hh * DN_DK:(hh + 1) * DN_DK] = qs * (lax.rsqrt(jnp.sum(qs * qs, axis=-1, keepdims=True) + EPS)
                                                       * (DN_DK ** -0.5))
        k_ref[0, :, hh * DN_DK:(hh + 1) * DN_DK] = ks * lax.rsqrt(jnp.sum(ks * ks, axis=-1, keepdims=True) + EPS)
    v_ref[0] = act[:, 2 * DN_W:3 * DN_W]

    ba = ba_ref[0]
    lane = lax.broadcasted_iota(jnp.int32, ba.shape, 1)
    beta = jax.nn.sigmoid(ba)
    g = -jnp.exp(al_ref[...]) * jax.nn.softplus(ba + dtb_ref[...])
    ri = lax.broadcasted_iota(jnp.int32, (T, T), 0)
    ci = lax.broadcasted_iota(jnp.int32, (T, T), 1)
    same = (ri // DN_CHUNK) == (ci // DN_CHUNK)
    tri_f = jnp.logical_and(same, ri >= ci).astype(f32)
    tri_b = jnp.logical_and(same, ri <= ci).astype(f32)
    gc_f = jnp.dot(tri_f, g, precision=HI, preferred_element_type=f32)
    gc_b = jnp.dot(tri_b, g, precision=HI, preferred_element_type=f32)
    nh = DN_HEADS
    gc = jnp.where(lane < 2 * nh + nh, gc_f, gc_b)
    gb = jnp.where(lane < 2 * nh, beta, jnp.where(lane < 4 * nh, gc, 0.0))
    gb_ref[0] = gb
    er = lax.broadcasted_iota(jnp.int32, (16, LANES), 0)
    ec = lax.broadcasted_iota(jnp.int32, (16, LANES), 1)
    sel = (er == ec).astype(f32)
    gbt = lax.dot_general(sel, gb, (((1,), (1,)), ((), ())), precision=HI, preferred_element_type=f32)
    for c in range(T // DN_CHUNK):
        gbt_ref[0, c] = gbt[:, c * DN_CHUNK:(c + 1) * DN_CHUNK]


def _dn_prepare(qkv, ba, conv_w, al_vec, dtb_vec, n_ctx_tiles):
    B, Lt, W = qkv.shape
    nt = Lt // TOK_TILE
    per8 = TOK_TILE // SUBLANES
    nblk8 = Lt // SUBLANES
    tok = lambda w: pl.BlockSpec((1, TOK_TILE, w), lambda b, t: (b, t, 0))
    body = functools.partial(_dn_prep_body, n_ctx_tiles=n_ctx_tiles, n_tiles=nt)
    return pl.pallas_call(
        body,
        out_shape=(jax.ShapeDtypeStruct((B, Lt, DN_W), f32),
                   jax.ShapeDtypeStruct((B, Lt, DN_W), f32),
                   jax.ShapeDtypeStruct((B, Lt, DN_W), f32),
                   jax.ShapeDtypeStruct((B, Lt, LANES), f32),
                   jax.ShapeDtypeStruct((B, Lt // DN_CHUNK, 16, DN_CHUNK), f32)),
        grid=(B, nt),
        in_specs=[tok(W),
                  pl.BlockSpec((1, SUBLANES, W), lambda b, t: (b, jnp.maximum(t * per8 - 1, 0), 0)),
                  pl.BlockSpec((1, SUBLANES, W), lambda b, t: (b, jnp.minimum((t + 1) * per8, nblk8 - 1), 0)),
                  tok(LANES),
                  _const_spec((SUBLANES, W)),
                  _const_spec((1, LANES)),
                  _const_spec((1, LANES))],
        out_specs=(tok(DN_W), tok(DN_W), tok(DN_W), tok(LANES),
                   pl.BlockSpec((1, TOK_TILE // DN_CHUNK, 16, DN_CHUNK), lambda b, t: (b, t, 0, 0))),
        scratch_shapes=[pltpu.VMEM((TOK_TILE + 2 * SUBLANES, W), f32)],
        compiler_params=_cparams(("arbitrary", "arbitrary")),
        name="dn_prepare",
    )(qkv, qkv, qkv, ba, conv_w, al_vec, dtb_vec)


def _unit_tri_inverse(n_mat, rev):
    C = n_mat.shape[0]
    ri = lax.broadcasted_iota(jnp.int32, (C, C), 0)
    ci = lax.broadcasted_iota(jnp.int32, (C, C), 1)
    t = (ri == ci).astype(f32)
    order = range(C - 1, 0, -1) if rev else range(0, C - 1)
    for j in order:
        t = t - n_mat[:, j:j + 1] * t[j:j + 1, :]
    return t


def _dn_chunk_body(*refs, rev):
    if rev:
        q_ref, k_ref, v_ref, gb_ref, gbt_ref, of_ref, o_ref, s_scr = refs
    else:
        q_ref, k_ref, v_ref, gb_ref, gbt_ref, o_ref, s_scr = refs
    C = DN_CHUNK
    d = 1 if rev else 0

    @pl.when(pl.program_id(1) == 0)
    def _():
        s_scr[...] = jnp.zeros_like(s_scr)

    ri = lax.broadcasted_iota(jnp.int32, (C, C), 0)
    ci = lax.broadcasted_iota(jnp.int32, (C, C), 1)
    incl = (ri <= ci) if rev else (ri >= ci)
    strict = (ri < ci) if rev else (ri > ci)
    last = 0 if rev else C - 1
    for hh in range(DN_HEADS):
        hs = slice(hh * DN_DK, (hh + 1) * DN_DK)
        bl = d * DN_HEADS + hh
        gl = 2 * DN_HEADS + bl
        q = q_ref[0, :, hs]
        k = k_ref[0, :, hs]
        v = v_ref[0, :, hs]
        beta = gb_ref[0, :, bl:bl + 1]
        gc = gb_ref[0, :, gl:gl + 1]
        gc_row = gbt_ref[0, 0, gl:gl + 1, :]
        decay = jnp.exp(jnp.where(incl, gc - gc_row, -jnp.inf))
        kb = k * beta
        n_mat = jnp.where(strict, _dot_nt(kb, k) * decay, 0.0)
        t_inv = _unit_tri_inverse(n_mat, rev)
        rhs = jnp.concatenate([v * beta, kb * jnp.exp(gc)], axis=1)
        sol = jnp.dot(t_inv, rhs, precision=HI, preferred_element_type=f32)
        u_c = sol[:, :DN_DV]
        w_c = sol[:, DN_DV:]
        qk = jnp.where(incl, _dot_nt(q, k) * decay, 0.0)
        s = s_scr[hh]
        v_new = u_c - _dot(w_c, s)
        o_h = _dot(q * jnp.exp(gc), s) + _dot(qk, v_new)
        g_end = gc[last:last + 1, :]
        k_dec = k * jnp.exp(g_end - gc)
        s_scr[hh] = s * jnp.exp(g_end) + _dot_tn(k_dec, v_new)
        if rev:
            o_h = o_h + of_ref[0, :, hs]
        o_ref[0, :, hs] = o_h


def _dn_chunks(q, k, v, gb, gbt, n_ctx_chunks, rev, o_fwd=None):
    B, Lt, W = q.shape
    nc = Lt // DN_CHUNK
    if rev:
        cidx = lambda i: jnp.where(i < n_ctx_chunks, n_ctx_chunks - 1 - i, nc - 1 - (i - n_ctx_chunks))
    else:
        cidx = lambda i: i
    tok = lambda w: pl.BlockSpec((1, DN_CHUNK, w), lambda b, i: (b, cidx(i), 0))
    in_specs = [tok(W), tok(W), tok(W), tok(LANES),
                pl.BlockSpec((1, 1, 16, DN_CHUNK), lambda b, i: (b, cidx(i), 0, 0))]
    args = [q, k, v, gb, gbt]
    if rev:
        in_specs.append(tok(W))
        args.append(o_fwd)
    return pl.pallas_call(
        functools.partial(_dn_chunk_body, rev=rev),
        out_shape=jax.ShapeDtypeStruct((B, Lt, W), f32),
        grid=(B, nc),
        in_specs=in_specs,
        out_specs=tok(W),
        scratch_shapes=[pltpu.VMEM((DN_HEADS, DN_DK, DN_DV), f32)],
        compiler_params=_cparams(("arbitrary", "arbitrary")),
        name="dn_chunks_bwd" if rev else "dn_chunks_fwd",
    )(*args)


def _attn_body(q_ref, kp_ref, kc_ref, kn_ref, kx_ref, vp_ref, vc_ref, vn_ref, vx_ref, sink_ref, o_ref,
               *, n_ctx_blocks, seq_len):
    Tq = AT_BLOCK
    G = AT_HEADS // AT_KV
    i = pl.program_id(1)
    is_ctx = i < n_ctx_blocks
    nloc = 3 * Tq
    qpos = (i - n_ctx_blocks) * Tq + lax.broadcasted_iota(jnp.int32, (Tq, nloc), 0)
    kpos = (i - n_ctx_blocks - 1) * Tq + lax.broadcasted_iota(jnp.int32, (Tq, nloc), 1)
    ok = jnp.logical_and(jnp.abs(qpos - kpos) <= WINDOW, jnp.logical_and(kpos >= 0, kpos < seq_len))
    ok = jnp.logical_and(ok, jnp.logical_not(is_ctx))
    lctx = kx_ref.shape[1]
    valid = jnp.concatenate([ok, jnp.ones((Tq, lctx), dtype=jnp.bool_)], axis=1)
    valid = jnp.concatenate([valid] * G, axis=0)
    scale = AT_HD ** -0.5
    for hk in range(AT_KV):
        ks = slice(hk * AT_HD, (hk + 1) * AT_HD)
        kall = jnp.concatenate([kp_ref[0, :, ks], kc_ref[0, :, ks], kn_ref[0, :, ks], kx_ref[0, :, ks]], axis=0)
        vall = jnp.concatenate([vp_ref[0, :, ks], vc_ref[0, :, ks], vn_ref[0, :, ks], vx_ref[0, :, ks]], axis=0)
        qs = jnp.concatenate([q_ref[0, :, (hk * G + g) * AT_HD:(hk * G + g + 1) * AT_HD] for g in range(G)],
                             axis=0)
        sink = jnp.concatenate([jnp.broadcast_to(sink_ref[0:1, hk * G + g:hk * G + g + 1], (Tq, 1))
                                for g in range(G)], axis=0)
        s = jnp.where(valid, _dot_nt(qs, kall) * scale, -jnp.inf)
        m = jnp.maximum(jnp.max(s, axis=-1, keepdims=True), sink)
        p = jnp.exp(s - m)
        denom = jnp.sum(p, axis=-1, keepdims=True) + jnp.exp(sink - m)
        o = _dot(p, vall) / denom
        for g in range(G):
            hq = hk * G + g
            o_ref[0, :, hq * AT_HD:(hq + 1) * AT_HD] = o[g * Tq:(g + 1) * Tq, :]


def _attention(aq, ak, av, sink_vec, n_ctx, seq_len):
    B, Lt, _ = aq.shape
    nb = Lt // AT_BLOCK
    n_ctx_blocks = n_ctx // AT_BLOCK
    blk = lambda w, f: pl.BlockSpec((1, AT_BLOCK, w), lambda b, i: (b, f(i), 0))
    lo, hi = n_ctx_blocks, nb - 1
    prev = lambda i: jnp.clip(i - 1, lo, hi)
    cur = lambda i: i
    nxt = lambda i: jnp.clip(i + 1, lo, hi)
    ctx_spec = pl.BlockSpec((1, n_ctx, AT_KW), lambda b, i: (b, 0, 0))
    body = functools.partial(_attn_body, n_ctx_blocks=n_ctx_blocks, seq_len=seq_len)
    return pl.pallas_call(
        body,
        out_shape=jax.ShapeDtypeStruct((B, Lt, AT_QW), f32),
        grid=(B, nb),
        in_specs=[blk(AT_QW, cur),
                  blk(AT_KW, prev), blk(AT_KW, cur), blk(AT_KW, nxt), ctx_spec,
                  blk(AT_KW, prev), blk(AT_KW, cur), blk(AT_KW, nxt), ctx_spec,
                  _const_spec((1, LANES))],
        out_specs=blk(AT_QW, cur),
        compiler_params=_cparams(("arbitrary", "arbitrary")),
        name="window_attn",
    )(aq, ak, ak, ak, ak, av, av, av, av, sink_vec)


def _merge_body(x_ref, mod_ref, g_ref, ys_ref, od_ref, z_ref, yc_ref,
                wg_ref, wglu_ref, ng_ref, wa_ref, wb_ref, wc_ref, wo_ref, o_ref):
    x = x_ref[0]
    h = _modnorm(x, g_ref[...], mod_ref[0, 1:2, :], mod_ref[0, 0:1, :]).astype(bf16)
    zs = jax.nn.gelu(ys_ref[...])
    ya = zs * jax.nn.sigmoid(_dot(zs, wglu_ref[...]))
    ng = ng_ref[...]
    parts = []
    for hh in range(DN_HEADS):
        hs = slice(hh * DN_DV, (hh + 1) * DN_DV)
        o = od_ref[0, :, hs]
        on = o * lax.rsqrt(jnp.mean(o * o, axis=-1, keepdims=True) + EPS) * ng
        parts.append(on * _silu(z_ref[0, :, hs]))
    yb = jnp.concatenate(parts, axis=1)
    yc = yc_ref[0]
    D = D_MODEL

    def gate(j):
        return jax.nn.sigmoid(jnp.dot(h, wg_ref[:, j * D:(j + 1) * D], preferred_element_type=f32))

    m = gate(0) * _dot(ya, wa_ref[...])
    m = m + gate(1) * _dot(yb, wb_ref[...])
    m = m + gate(2) * _dot(yc, wc_ref[...])
    mix = _dot(m, wo_ref[...])
    o_ref[0] = x + mod_ref[0, 2:3, :] * mix


def _merge(xcat, mods, g1, ys_tm, o_dn, dz, yc, w_gate, w_glu, dn_g, w_a, w_b, w_c, w_o, n_ctx_tiles):
    B, Lt, D = xcat.shape
    nt = Lt // TOK_TILE
    tok = lambda w: pl.BlockSpec((1, TOK_TILE, w), lambda b, t: (b, t, 0))
    return pl.pallas_call(
        _merge_body,
        out_shape=jax.ShapeDtypeStruct((B, Lt, D), f32),
        grid=(B, nt),
        in_specs=[tok(D),
                  pl.BlockSpec((1, N_MOD, D), _mod_index(n_ctx_tiles, B)),
                  _const_spec((1, D)),
                  pl.BlockSpec((TOK_TILE, SSM_WIDTH), lambda b, t: (t, b)),
                  tok(DN_W), tok(DN_W), tok(AT_QW),
                  _const_spec(w_gate.shape), _const_spec(w_glu.shape), _const_spec((1, DN_DV)),
                  _const_spec(w_a.shape), _const_spec(w_b.shape), _const_spec(w_c.shape),
                  _const_spec(w_o.shape)],
        out_specs=tok(D),
        compiler_params=_cparams(("arbitrary", "arbitrary")),
        name="merge_out",
    )(xcat, mods, g1, ys_tm, o_dn, dz, yc, w_gate, w_glu, dn_g, w_a, w_b, w_c, w_o)


FF_CHUNK = 1024


def _ffn_body(x_ref, mod_ref, g_ref, w1_ref, w2_ref, o_ref):
    x = x_ref[0]
    h = _modnorm(x, g_ref[...], mod_ref[0, 4:5, :], mod_ref[0, 3:4, :]).astype(bf16)
    acc = jnp.zeros(x.shape, f32)
    for j in range(D_FF // FF_CHUNK):
        a = jnp.dot(h, w1_ref[:, j * FF_CHUNK:(j + 1) * FF_CHUNK], preferred_element_type=f32)
        a = jnp.square(jnp.maximum(a, 0.0)).astype(bf16)
        acc = acc + jnp.dot(a, w2_ref[j * FF_CHUNK:(j + 1) * FF_CHUNK, :], preferred_element_type=f32)
    o_ref[0] = x + mod_ref[0, 5:6, :] * acc


def _ffn(xcat, mods, g2, w1, w2, n_ctx_tiles):
    B, Lt, D = xcat.shape
    nt = Lt // TOK_TILE
    tok = pl.BlockSpec((1, TOK_TILE, D), lambda b, t: (b, t, 0))
    return pl.pallas_call(
        _ffn_body,
        out_shape=jax.ShapeDtypeStruct((B, Lt, D), f32),
        grid=(B, nt),
        in_specs=[tok, pl.BlockSpec((1, N_MOD, D), _mod_index(n_ctx_tiles, B)), _const_spec((1, D)),
                  _const_spec(w1.shape), _const_spec(w2.shape)],
        out_specs=tok,
        compiler_params=_cparams(("arbitrary", "arbitrary")),
        name="ffn",
    )(xcat, mods, g2, w1, w2)


def _final_body(x_ref, g_ref, o_ref):
    x = x_ref[0]
    o_ref[0] = x * lax.rsqrt(jnp.mean(x * x, axis=-1, keepdims=True) + EPS) * g_ref[...]


def _final_norm(xcat, g, n_ctx_tiles, seq_len):
    B, Lt, D = xcat.shape
    return pl.pallas_call(
        _final_body,
        out_shape=jax.ShapeDtypeStruct((B, seq_len, D), f32),
        grid=(B, seq_len // TOK_TILE),
        in_specs=[pl.BlockSpec((1, TOK_TILE, D), lambda b, t: (b, t + n_ctx_tiles, 0)), _const_spec((1, D))],
        out_specs=pl.BlockSpec((1, TOK_TILE, D), lambda b, t: (b, t, 0)),
        compiler_params=_cparams(("arbitrary", "arbitrary")),
        name="final_norm",
    )(xcat, g)


def _rope_tables(n_ctx, seq_len):
    n = AT_HD // 4
    inv_freq = ROPE_BASE ** (-jnp.arange(n, dtype=f32) / n)
    pos = jnp.arange(seq_len, dtype=jnp.int32)
    rows = (pos // GRID_W).astype(f32)[:, None] * inv_freq[None, :]
    cols = (pos % GRID_W).astype(f32)[:, None] * inv_freq[None, :]
    cos = jnp.concatenate([jnp.cos(rows)] * 2 + [jnp.cos(cols)] * 2, axis=1)
    sin = jnp.concatenate([-jnp.sin(rows), jnp.sin(rows), -jnp.sin(cols), jnp.sin(cols)], axis=1)
    cos = jnp.concatenate([jnp.ones((n_ctx, AT_HD), f32), cos], axis=0)
    sin = jnp.concatenate([jnp.zeros((n_ctx, AT_HD), f32), sin], axis=0)
    return jnp.tile(cos, (1, LANES // AT_HD)), jnp.tile(sin, (1, LANES // AT_HD))


def _split_w_in(w_in):
    o = SSM_WIDTH + 4 * DN_W
    w_ba = w_in[:, o:o + 4 * DN_HEADS]
    o2 = o + 4 * DN_HEADS
    w_att = w_in[:, o2:o2 + AT_QW + 2 * AT_KW]
    w_gate = w_in[:, o2 + AT_QW + 2 * AT_KW:]
    w_main = jnp.concatenate([w_in[:, :o], w_att], axis=1).astype(bf16)
    w_ba = jnp.pad(w_ba, ((0, 0), (0, LANES - 4 * DN_HEADS))).astype(bf16)
    return w_main, w_ba, w_gate.astype(bf16)


def kernel(x, c, ctx, c_ctx, norm1_g, norm2_g, w_mod, b_mod, w_in, ssm_lam_re, ssm_lam_im, ssm_log_dt, ssm_b_re, ssm_b_im, ssm_c_re, ssm_c_im, ssm_d, ssm_w_glu, dn_conv_w, dn_a_log, dn_dt_bias, dn_norm_g, attn_sink, w_branch_a, w_branch_b, w_branch_c, w_out, w_ff1, w_ff2, final_norm_g):
    B, L, D = x.shape
    Lc = ctx.shape[1]
    depth = w_in.shape[0]
    assert B == SUBLANES and D == D_MODEL
    assert Lc % TOK_TILE == 0 and L % TOK_TILE == 0
    Lt = Lc + L
    n_ctx_tiles = Lc // TOK_TILE

    cond = jnp.concatenate([c, c_ctx[None, :], jnp.zeros((16 - B - 1, D), f32)], axis=0)
    mods_all = _modulation(cond, w_mod, b_mod).reshape(depth, 16, N_MOD, D)
    cos_t, sin_t = _rope_tables(Lc, L)
    xcat = jnp.concatenate([ctx, x], axis=1)

    for layer in range(depth):
        mods = mods_all[layer]
        g1 = norm1_g[layer].reshape(1, D)
        g2 = norm2_g[layer].reshape(1, D)
        w_main, w_ba, w_gate = _split_w_in(w_in[layer])

        u_tm, dqkv, dz, dba, aq, ak, av = _inproj(xcat, mods, g1, w_main, w_ba, cos_t, sin_t, n_ctx_tiles)

        a_re, a_im, bbt_re, bbt_im = _s5_discretize(ssm_lam_re[layer], ssm_lam_im[layer], ssm_log_dt[layer],
                                                    ssm_b_re[layer], ssm_b_im[layer])
        bmat, cmat, avec = _s5_matrices(a_re, a_im, bbt_re, bbt_im, ssm_c_re[layer], ssm_c_im[layer])
        u_rows = u_tm.reshape(Lt * B, SSM_WIDTH)
        y_f = _s5_scan(u_rows, bmat, cmat, avec, Lc, rev=False)
        y_s5 = _s5_scan(u_rows, bmat, cmat, avec, Lc, rev=True, yf=y_f,
                        d_skip=ssm_d[layer].reshape(1, SSM_WIDTH))
        ys_tm = y_s5.reshape(Lt, B * SSM_WIDTH)

        conv_w = jnp.pad(dn_conv_w[layer], ((0, SUBLANES - DN_CONV), (0, 0)))
        pad_l = 2 * DN_HEADS
        al_vec = jnp.pad(dn_a_log[layer].reshape(1, 2 * DN_HEADS), ((0, 0), (pad_l, LANES - 2 * pad_l)))
        dtb_vec = jnp.pad(dn_dt_bias[layer].reshape(1, 2 * DN_HEADS), ((0, 0), (pad_l, LANES - 2 * pad_l)))
        qn, kn, vs, gb, gbt = _dn_prepare(dqkv, dba, conv_w, al_vec, dtb_vec, n_ctx_tiles)
        n_ctx_chunks = Lc // DN_CHUNK
        o_f = _dn_chunks(qn, kn, vs, gb, gbt, n_ctx_chunks, rev=False)
        o_dn = _dn_chunks(qn, kn, vs, gb, gbt, n_ctx_chunks, rev=True, o_fwd=o_f)

        sink_vec = jnp.pad(attn_sink[layer].reshape(1, AT_HEADS), ((0, 0), (0, LANES - AT_HEADS)))
        yc = _attention(aq, ak, av, sink_vec, Lc, L)

        x1 = _merge(xcat, mods, g1, ys_tm, o_dn, dz, yc, w_gate, ssm_w_glu[layer].astype(bf16),
                    dn_norm_g[layer].reshape(1, DN_DV), w_branch_a[layer].astype(bf16),
                    w_branch_b[layer].astype(bf16), w_branch_c[layer].astype(bf16),
                    w_out[layer].astype(bf16), n_ctx_tiles)
        xcat = _ffn(x1, mods, g2, w_ff1[layer].astype(bf16), w_ff2[layer].astype(bf16), n_ctx_tiles)

    return _final_norm(xcat, final_norm_g.reshape(1, D), n_ctx_tiles, L)
```

```python
import functools
import math

import jax
import jax.numpy as jnp
from jax import lax
from jax.experimental import pallas as pl
from jax.experimental.pallas import tpu as pltpu

f32 = jnp.float32
bf16 = jnp.bfloat16
HI = lax.Precision.HIGHEST

D_MODEL = 1024
GRID_W = 64
EPS = 1e-6
SSM_WIDTH = D_MODEL // 2
SSM_GROUP = 16
SSM_GROUPS = SSM_WIDTH // SSM_GROUP
SSM_STATE = 64
DN_HEADS = 4
DN_DK = 128
DN_DV = 128
DN_CONV = 5
DN_CHUNK = 64
AT_HEADS = 8
AT_KV = 2
AT_HD = 64
WINDOW = 128
AT_BLOCK = 128
ROPE_BASE = 10000.0
D_FF = 4 * D_MODEL
N_MOD = 6
DN_W = DN_HEADS * DN_DK
AT_QW = AT_HEADS * AT_HD
AT_KW = AT_KV * AT_HD

SUBLANES = 8
LANES = 128
TOK_TILE = 256
S5_STEPS = 32
S5_COLBLK = 128
S5_NBLK = SSM_WIDTH // S5_COLBLK
S5_SBLK = (S5_COLBLK // SSM_GROUP) * SSM_STATE
VMEM_LIMIT = 56 * 1024 * 1024


def _cparams(sem):
    return pltpu.CompilerParams(dimension_semantics=sem, vmem_limit_bytes=VMEM_LIMIT)


def _const_spec(shape):
    nd = len(shape)
    return pl.BlockSpec(shape, lambda *_: (0,) * nd, pipeline_mode=pl.Buffered(1))


def _dot(a, b):
    return jnp.dot(a.astype(bf16), b.astype(bf16), preferred_element_type=f32)


def _dot_nt(a, b):
    return lax.dot_general(a.astype(bf16), b.astype(bf16), (((1,), (1,)), ((), ())),
                           preferred_element_type=f32)


def _dot_tn(a, b):
    return lax.dot_general(a.astype(bf16), b.astype(bf16), (((0,), (0,)), ((), ())),
                           preferred_element_type=f32)


def _modnorm(x, g, scale, shift):
    y = x * lax.rsqrt(jnp.mean(x * x, axis=-1, keepdims=True) + EPS)
    return (y * g) * (1.0 + scale) + shift


def _silu(x):
    return x * jax.nn.sigmoid(x)


def _mod_body(c_ref, w_ref, b_ref, o_ref):
    s = _silu(c_ref[...])
    o_ref[0] = jnp.dot(s, w_ref[0], precision=HI, preferred_element_type=f32) + b_ref[0]


def _modulation(cond, w_mod, b_mod):
    depth = w_mod.shape[0]
    nblk = (N_MOD * D_MODEL) // D_MODEL
    return pl.pallas_call(
        _mod_body,
        out_shape=jax.ShapeDtypeStruct((depth, 16, N_MOD * D_MODEL), f32),
        grid=(depth, nblk),
        in_specs=[pl.BlockSpec((16, D_MODEL), lambda l, j: (0, 0)),
                  pl.BlockSpec((1, D_MODEL, D_MODEL), lambda l, j: (l, 0, j)),
                  pl.BlockSpec((1, 1, D_MODEL), lambda l, j: (l, 0, j))],
        out_specs=pl.BlockSpec((1, 16, D_MODEL), lambda l, j: (l, 0, j)),
        compiler_params=_cparams(("arbitrary", "arbitrary")),
        name="adaln_mod",
    )(cond, w_mod, b_mod.reshape(depth, 1, N_MOD * D_MODEL))


def _rope(x, cos, sin):
    n = x.shape[-1]
    lane = lax.broadcasted_iota(jnp.int32, x.shape, 1)
    partner = jnp.where((lane % 32) < 16, pltpu.roll(x, n - 16, 1), pltpu.roll(x, 16, 1))
    return x * cos + partner * sin


def _inproj_body(x_ref, mod_ref, g_ref, wm_ref, wba_ref, cos_ref, sin_ref,
                 u_ref, qkv_ref, z_ref, ba_ref, aq_ref, ak_ref, av_ref):
    h = _modnorm(x_ref[0], g_ref[...], mod_ref[0, 1:2, :], mod_ref[0, 0:1, :]).astype(bf16)

    def proj(lo, hi):
        return jnp.dot(h, wm_ref[:, lo:hi], preferred_element_type=f32)

    o = 0
    u_ref[...] = proj(o, o + SSM_WIDTH)
    o += SSM_WIDTH
    qkv_ref[0] = proj(o, o + 3 * DN_W)
    o += 3 * DN_W
    z_ref[0] = proj(o, o + DN_W)
    o += DN_W
    cos = cos_ref[...]
    sin = sin_ref[...]
    reps = AT_QW // LANES
    aq_ref[0] = _rope(proj(o, o + AT_QW), jnp.concatenate([cos] * reps, axis=1),
                      jnp.concatenate([sin] * reps, axis=1))
    o += AT_QW
    ak_ref[0] = _rope(proj(o, o + AT_KW), cos, sin)
    o += AT_KW
    av_ref[0] = proj(o, o + AT_KW)
    ba_ref[0] = jnp.dot(h, wba_ref[...], preferred_element_type=f32)


def _mod_index(n_ctx_tiles, ctx_row):
    return lambda b, t: (jnp.where(t < n_ctx_tiles, ctx_row, b), 0, 0)


def _inproj(xcat, mods, g1, w_main, w_ba, cos_t, sin_t, n_ctx_tiles):
    B, Lt, D = xcat.shape
    nt = Lt // TOK_TILE
    wm = w_main.shape[1]
    tok = lambda w: pl.BlockSpec((1, TOK_TILE, w), lambda b, t: (b, t, 0))
    out_shape = (
        jax.ShapeDtypeStruct((Lt, B * SSM_WIDTH), f32),
        jax.ShapeDtypeStruct((B, Lt, 3 * DN_W), f32),
        jax.ShapeDtypeStruct((B, Lt, DN_W), f32),
        jax.ShapeDtypeStruct((B, Lt, LANES), f32),
        jax.ShapeDtypeStruct((B, Lt, AT_QW), f32),
        jax.ShapeDtypeStruct((B, Lt, AT_KW), f32),
        jax.ShapeDtypeStruct((B, Lt, AT_KW), f32),
    )
    return pl.pallas_call(
        _inproj_body,
        out_shape=out_shape,
        grid=(B, nt),
        in_specs=[tok(D),
                  pl.BlockSpec((1, N_MOD, D), _mod_index(n_ctx_tiles, B)),
                  _const_spec((1, D)),
                  _const_spec((D, wm)),
                  _const_spec((D, LANES)),
                  pl.BlockSpec((TOK_TILE, LANES), lambda b, t: (t, 0)),
                  pl.BlockSpec((TOK_TILE, LANES), lambda b, t: (t, 0))],
        out_specs=(pl.BlockSpec((TOK_TILE, SSM_WIDTH), lambda b, t: (t, b)),
                   tok(3 * DN_W), tok(DN_W), tok(LANES), tok(AT_QW), tok(AT_KW), tok(AT_KW)),
        compiler_params=_cparams(("arbitrary", "arbitrary")),
        name="in_proj",
    )(xcat, mods, g1, w_main, w_ba, cos_t, sin_t)


def _s5disc_body(lre_ref, lim_ref, ldt_ref, bre_ref, bim_ref, are_ref, aim_ref, bbre_ref, bbim_ref):
    lr = lre_ref[0]
    li = lim_ref[0]
    dt = jnp.exp(ldt_ref[0])
    mag = jnp.exp(lr * dt)
    a_re = mag * jnp.cos(li * dt)
    a_im = mag * jnp.sin(li * dt)
    den = lr * lr + li * li
    f_re = ((a_re - 1.0) * lr + a_im * li) / den
    f_im = (a_im * lr - (a_re - 1.0) * li) / den
    are_ref[0] = a_re
    aim_ref[0] = a_im
    b_re = bre_ref[0]
    b_im = bim_ref[0]
    bbre_ref[0] = f_re * b_re - f_im * b_im
    bbim_ref[0] = f_re * b_im + f_im * b_re


def _s5_discretize(lam_re, lam_im, log_dt, b_re, b_im):
    n = 2 * SSM_GROUPS
    P, H = SSM_STATE, SSM_GROUP
    vec = pl.BlockSpec((1, 1, P), lambda i: (i, 0, 0))
    mat = pl.BlockSpec((1, H, P), lambda i: (i, 0, 0))
    return pl.pallas_call(
        _s5disc_body,
        out_shape=(jax.ShapeDtypeStruct((n, 1, P), f32), jax.ShapeDtypeStruct((n, 1, P), f32),
                   jax.ShapeDtypeStruct((n, H, P), f32), jax.ShapeDtypeStruct((n, H, P), f32)),
        grid=(n,),
        in_specs=[vec, vec, pl.BlockSpec((1, 1, 1), lambda i: (i, 0, 0)), mat, mat],
        out_specs=(vec, vec, mat, mat),
        compiler_params=_cparams(("arbitrary",)),
        name="s5_discretize",
    )(lam_re.reshape(n, 1, P), lam_im.reshape(n, 1, P), log_dt.reshape(n, 1, 1),
      jnp.swapaxes(b_re, -1, -2).reshape(n, H, P), jnp.swapaxes(b_im, -1, -2).reshape(n, H, P))


def _s5_matrices(a_re, a_im, bbt_re, bbt_im, c_re, c_im):
    gb = S5_COLBLK // SSM_GROUP
    P, H = SSM_STATE, SSM_GROUP
    eye = jnp.eye(gb, dtype=f32)

    def bblk(t):
        t = t.reshape(2, S5_NBLK, gb, H, P)
        return jnp.einsum('dcghp,gk->dcghkp', t, eye).reshape(2, S5_NBLK, gb * H, gb * P)

    def cblk(t):
        t = t.reshape(2, S5_NBLK, gb, H, P)
        return jnp.einsum('dcghp,gk->dcgpkh', t, eye).reshape(2, S5_NBLK, gb * P, gb * H)

    bmat = jnp.concatenate([bblk(bbt_re), bblk(bbt_im)], axis=-1).astype(bf16)
    cmat = jnp.concatenate([cblk(c_re.astype(f32)), -cblk(c_im.astype(f32))], axis=-2).astype(bf16)
    avec = jnp.stack([a_re.reshape(2, S5_NBLK, gb * P), a_im.reshape(2, S5_NBLK, gb * P)], axis=2)
    return bmat, cmat, avec


def _s5_scan_body(*refs, rev):
    if rev:
        u_ref, bm_ref, cm_ref, a_ref, yf_ref, d_ref, y_ref, bu_scr, st_scr = refs
    else:
        u_ref, bm_ref, cm_ref, a_ref, y_ref, bu_scr, st_scr = refs
    B = SUBLANES
    nsb = S5_SBLK

    @pl.when(pl.program_id(0) == 0)
    def _():
        st_scr[...] = jnp.zeros_like(st_scr)

    for cb in range(S5_NBLK):
        cols = slice(cb * S5_COLBLK, (cb + 1) * S5_COLBLK)
        bu_scr[...] = jnp.dot(u_ref[:, cols].astype(bf16), bm_ref[0, cb], preferred_element_type=f32)
        are = jnp.broadcast_to(a_ref[0, cb, 0:1, :], (B, nsb))
        aim = jnp.broadcast_to(a_ref[0, cb, 1:2, :], (B, nsb))

        def step(i, carry, are=are, aim=aim):
            sre, sim = carry
            t = (S5_STEPS - 1 - i) if rev else i
            r0 = pl.multiple_of(t * B, B)
            nre = are * sre - aim * sim + bu_scr[pl.ds(r0, B), 0:nsb]
            nim = are * sim + aim * sre + bu_scr[pl.ds(r0, B), nsb:2 * nsb]
            bu_scr[pl.ds(r0, B), 0:nsb] = nre
            bu_scr[pl.ds(r0, B), nsb:2 * nsb] = nim
            return nre, nim

        sre, sim = lax.fori_loop(0, S5_STEPS, step, (st_scr[cb, :, 0:nsb], st_scr[cb, :, nsb:2 * nsb]),
                                 unroll=4)
        st_scr[cb, :, 0:nsb] = sre
        st_scr[cb, :, nsb:2 * nsb] = sim
        y = jnp.dot(bu_scr[...].astype(bf16), cm_ref[0, cb], preferred_element_type=f32)
        if rev:
            y = y + yf_ref[:, cols] + d_ref[:, cols] * u_ref[:, cols]
        y_ref[:, cols] = y


def _s5_tile_index(n_ctx_tiles, n_tiles, rev):
    if not rev:
        return lambda i: (i, 0)
    return lambda i: (jnp.where(i < n_ctx_tiles, n_ctx_tiles - 1 - i, n_tiles - 1 - (i - n_ctx_tiles)), 0)


def _s5_scan(u_rows, bmat, cmat, avec, n_ctx_steps, rev, yf=None, d_skip=None):
    n_rows = u_rows.shape[0]
    rows = S5_STEPS * SUBLANES
    n_tiles = n_rows // rows
    n_ctx_tiles = n_ctx_steps // S5_STEPS
    idx = _s5_tile_index(n_ctx_tiles, n_tiles, rev)
    d = 1 if rev else 0
    tile = pl.BlockSpec((rows, SSM_WIDTH), idx)
    in_specs = [tile,
                pl.BlockSpec((1, S5_NBLK, S5_COLBLK, 2 * S5_SBLK), lambda i: (d, 0, 0, 0), pipeline_mode=pl.Buffered(1)),
                pl.BlockSpec((1, S5_NBLK, 2 * S5_SBLK, S5_COLBLK), lambda i: (d, 0, 0, 0), pipeline_mode=pl.Buffered(1)),
                pl.BlockSpec((1, S5_NBLK, 2, S5_SBLK), lambda i: (d, 0, 0, 0), pipeline_mode=pl.Buffered(1))]
    args = [u_rows, bmat, cmat, avec]
    if rev:
        in_specs += [tile, _const_spec((1, SSM_WIDTH))]
        args += [yf, d_skip]
    return pl.pallas_call(
        functools.partial(_s5_scan_body, rev=rev),
        out_shape=jax.ShapeDtypeStruct((n_rows, SSM_WIDTH), f32),
        grid=(n_tiles,),
        in_specs=in_specs,
        out_specs=tile,
        scratch_shapes=[pltpu.VMEM((rows, 2 * S5_SBLK), f32),
                        pltpu.VMEM((S5_NBLK, SUBLANES, 2 * S5_SBLK), f32)],
        compiler_params=_cparams(("arbitrary",)),
        name="s5_scan_bwd" if rev else "s5_scan_fwd",
    )(*args)


def _dn_prep_body(qkv_ref, prev_ref, next_ref, ba_ref, cw_ref, al_ref, dtb_ref,
                  q_ref, k_ref, v_ref, gb_ref, gbt_ref, ext_scr, *, n_ctx_tiles, n_tiles):
    T = TOK_TILE
    t = pl.program_id(1)
    prev_ok = jnp.logical_and(t != 0, t != n_ctx_tiles)
    next_ok = jnp.logical_and(t != n_ctx_tiles - 1, t != n_tiles - 1)
    ext_scr[0:SUBLANES, :] = jnp.where(prev_ok, prev_ref[0], 0.0)
    ext_scr[SUBLANES:SUBLANES + T, :] = qkv_ref[0]
    ext_scr[SUBLANES + T:2 * SUBLANES + T, :] = jnp.where(next_ok, next_ref[0], 0.0)
    half = DN_CONV // 2
    acc = cw_ref[0:1, :] * ext_scr[pl.ds(SUBLANES - half, T), :]
    for kk in range(1, DN_CONV):
        acc = acc + cw_ref[kk:kk + 1, :] * ext_scr[pl.ds(SUBLANES - half + kk, T), :]
    act = _silu(acc)
    for hh in range(DN_HEADS):
        qs = act[:, hh * DN_DK:(hh + 1) * DN_DK]
        ks = act[:, DN_W + hh * DN_DK:DN_W + (hh + 1) * DN_DK]
        q_ref[0, :, hh * DN_DK:(hh + 1) * DN_DK] = qs * (lax.rsqrt(jnp.sum(qs * qs, axis=-1, keepdims=True) + EPS)
                                                       * (DN_DK ** -0.5))
        k_ref[0, :, hh * DN_DK:(hh + 1) * DN_DK] = ks * lax.rsqrt(jnp.sum(ks * ks, axis=-1, keepdims=True) + EPS)
    v_ref[0] = act[:, 2 * DN_W:3 * DN_W]

    ba = ba_ref[0]
    lane = lax.broadcasted_iota(jnp.int32, ba.shape, 1)
    beta = jax.nn.sigmoid(ba)
    g = -jnp.exp(al_ref[...]) * jax.nn.softplus(ba + dtb_ref[...])
    ri = lax.broadcasted_iota(jnp.int32, (T, T), 0)
    ci = lax.broadcasted_iota(jnp.int32, (T, T), 1)
    same = (ri // DN_CHUNK) == (ci // DN_CHUNK)
    tri_f = jnp.logical_and(same, ri >= ci).astype(f32)
    tri_b = jnp.logical_and(same, ri <= ci).astype(f32)
    gc_f = jnp.dot(tri_f, g, precision=HI, preferred_element_type=f32)
    gc_b = jnp.dot(tri_b, g, precision=HI, preferred_element_type=f32)
    nh = DN_HEADS
    gc = jnp.where(lane < 2 * nh + nh, gc_f, gc_b)
    gb = jnp.where(lane < 2 * nh, beta, jnp.where(lane < 4 * nh, gc, 0.0))
    gb_ref[0] = gb
    er = lax.broadcasted_iota(jnp.int32, (16, LANES), 0)
    ec = lax.broadcasted_iota(jnp.int32, (16, LANES), 1)
    sel = (er == ec).astype(f32)
    gbt = lax.dot_general(sel, gb, (((1,), (1,)), ((), ())), precision=HI, preferred_element_type=f32)
    for c in range(T // DN_CHUNK):
        gbt_ref[0, c] = gbt[:, c * DN_CHUNK:(c + 1) * DN_CHUNK]


def _dn_prepare(qkv, ba, conv_w, al_vec, dtb_vec, n_ctx_tiles):
    B, Lt, W = qkv.shape
    nt = Lt // TOK_TILE
    per8 = TOK_TILE // SUBLANES
    nblk8 = Lt // SUBLANES
    tok = lambda w: pl.BlockSpec((1, TOK_TILE, w), lambda b, t: (b, t, 0))
    body = functools.partial(_dn_prep_body, n_ctx_tiles=n_ctx_tiles, n_tiles=nt)
    return pl.pallas_call(
        body,
        out_shape=(jax.ShapeDtypeStruct((B, Lt, DN_W), f32),
                   jax.ShapeDtypeStruct((B, Lt, DN_W), f32),
                   jax.ShapeDtypeStruct((B, Lt, DN_W), f32),
                   jax.ShapeDtypeStruct((B, Lt, LANES), f32),
                   jax.ShapeDtypeStruct((B, Lt // DN_CHUNK, 16, DN_CHUNK), f32)),
        grid=(B, nt),
        in_specs=[tok(W),
                  pl.BlockSpec((1, SUBLANES, W), lambda b, t: (b, jnp.maximum(t * per8 - 1, 0), 0)),
                  pl.BlockSpec((1, SUBLANES, W), lambda b, t: (b, jnp.minimum((t + 1) * per8, nblk8 - 1), 0)),
                  tok(LANES),
                  _const_spec((SUBLANES, W)),
                  _const_spec((1, LANES)),
                  _const_spec((1, LANES))],
        out_specs=(tok(DN_W), tok(DN_W), tok(DN_W), tok(LANES),
                   pl.BlockSpec((1, TOK_TILE // DN_CHUNK, 16, DN_CHUNK), lambda b, t: (b, t, 0, 0))),
        scratch_shapes=[pltpu.VMEM((TOK_TILE + 2 * SUBLANES, W), f32)],
        compiler_params=_cparams(("arbitrary", "arbitrary")),
        name="dn_prepare",
    )(qkv, qkv, qkv, ba, conv_w, al_vec, dtb_vec)


def _unit_tri_inverses(n_scr, bc_scr, revs):
    C = DN_CHUNK
    ng = C // SUBLANES
    ri = lax.broadcasted_iota(jnp.int32, (SUBLANES, 2 * C), 0)
    ci = lax.broadcasted_iota(jnp.int32, (SUBLANES, 2 * C), 1)
    eye = [(ri + SUBLANES * g == ci % C).astype(f32) for g in range(ng)]

    def pivot(step, rev):
        j = C - 1 - step if rev else step
        gj = j // SUBLANES
        return j, gj, (range(0, gj + 1) if rev else range(gj, ng))

    for step in range(C - 1):
        for p, rev in enumerate(revs):
            j, _, groups = pivot(step, rev)
            idx = jnp.where(ci < C, j, C + j)
            for g in groups:
                rows = slice(SUBLANES * g, SUBLANES * (g + 1))
                bc_scr[p, step, rows, :] = jnp.take_along_axis(n_scr[p, rows, :], idx, axis=1)
    ts = [list(eye) for _ in revs]
    for step in range(C - 1):
        for p, rev in enumerate(revs):
            j, gj, groups = pivot(step, rev)
            row = ts[p][gj][j % SUBLANES:j % SUBLANES + 1, :]
            for g in groups:
                ts[p][g] = ts[p][g] - bc_scr[p, step, SUBLANES * g:SUBLANES * (g + 1), :] * row
    return [jnp.concatenate(t, axis=0) for t in ts]


def _split3_dot(a, b):
    a_hi = a.astype(bf16)
    a_lo = (a - a_hi.astype(f32)).astype(bf16)
    b_hi = b.astype(bf16)
    b_lo = (b - b_hi.astype(f32)).astype(bf16)
    return jnp.dot(jnp.concatenate([a_hi, a_hi, a_lo], axis=1), jnp.concatenate([b_hi, b_lo, b_hi], axis=0),
                   preferred_element_type=f32)


def _dn_intra_body(q_ref, k_ref, v_ref, gb_ref, gbt_ref, uc_ref, wc_ref, qe_ref, kdt_ref, qk_ref, n_scr, bc_scr):
    C = DN_CHUNK
    ri = lax.broadcasted_iota(jnp.int32, (C, C), 0)
    ci = lax.broadcasted_iota(jnp.int32, (C, C), 1)
    incl = (ri >= ci, ri <= ci)
    strict = (ri > ci, ri < ci)
    last = (C - 1, 0)
    revs = [False, True] * (DN_HEADS // 2)
    meta = []
    for hh in range(DN_HEADS):
        hs = slice(hh * DN_DK, (hh + 1) * DN_DK)
        q = q_ref[0, :, hs]
        k = k_ref[0, :, hs]
        kk = _dot_nt(k, k)
        qk0 = _dot_nt(q, k)
        kt = k.T
        for d in range(2):
            bl = d * DN_HEADS + hh
            gl = 2 * DN_HEADS + bl
            beta = gb_ref[0, :, bl:bl + 1]
            gc = gb_ref[0, :, gl:gl + 1]
            gc_row = gbt_ref[0, 0, gl:gl + 1, :]
            decay = jnp.exp(jnp.where(incl[d], gc - gc_row, -jnp.inf))
            half = hh % 2
            n_scr[(hh // 2) * 2 + d, :, half * C:(half + 1) * C] = jnp.where(strict[d], (beta * kk) * decay, 0.0)
            qk_ref[d, 0, 0, hh] = jnp.where(incl[d], qk0 * decay, 0.0).astype(bf16)
            g_end = gc[last[d]:last[d] + 1, :]
            qe_ref[d, 0, :, hs] = (q * jnp.exp(gc)).astype(bf16)
            kdt_ref[d, 0, 0, hs, :] = (kt * jnp.exp(g_end - gc_row)).astype(bf16)
            meta.append((hh, d, hs, beta, gc))
    t_pairs = _unit_tri_inverses(n_scr, bc_scr, revs)
    for hh, d, hs, beta, gc in meta:
        half = hh % 2
        t_inv = t_pairs[(hh // 2) * 2 + d][:, half * C:(half + 1) * C]
        k = k_ref[0, :, hs]
        v = v_ref[0, :, hs]
        rhs = jnp.concatenate([v * beta, k * (beta * jnp.exp(gc))], axis=1)
        sol = _split3_dot(t_inv, rhs)
        uc_ref[d, 0, :, hs] = sol[:, :DN_DV]
        wc_ref[d, 0, :, hs] = sol[:, DN_DV:].astype(bf16)


def _dn_intra(q, k, v, gb, gbt):
    B, Lt, W = q.shape
    nc = Lt // DN_CHUNK
    tok = lambda w: pl.BlockSpec((1, DN_CHUNK, w), lambda b, i: (b, i, 0))
    tok2 = lambda w: pl.BlockSpec((2, 1, DN_CHUNK, w), lambda b, i: (0, b, i, 0))
    return pl.pallas_call(
        _dn_intra_body,
        out_shape=(jax.ShapeDtypeStruct((2, B, Lt, W), f32),
                   jax.ShapeDtypeStruct((2, B, Lt, W), bf16),
                   jax.ShapeDtypeStruct((2, B, Lt, W), bf16),
                   jax.ShapeDtypeStruct((2, B, nc, W, DN_CHUNK), bf16),
                   jax.ShapeDtypeStruct((2, B, nc, DN_HEADS, DN_CHUNK, DN_CHUNK), bf16)),
        grid=(B, nc),
        in_specs=[tok(W), tok(W), tok(W), tok(LANES),
                  pl.BlockSpec((1, 1, 16, DN_CHUNK), lambda b, i: (b, i, 0, 0))],
        out_specs=(tok2(W), tok2(W), tok2(W),
                   pl.BlockSpec((2, 1, 1, W, DN_CHUNK), lambda b, i: (0, b, i, 0, 0)),
                   pl.BlockSpec((2, 1, 1, DN_HEADS, DN_CHUNK, DN_CHUNK), lambda b, i: (0, b, i, 0, 0, 0))),
        scratch_shapes=[pltpu.VMEM((DN_HEADS, DN_CHUNK, 2 * DN_CHUNK), f32),
                        pltpu.VMEM((DN_HEADS, DN_CHUNK - 1, DN_CHUNK, 2 * DN_CHUNK), f32)],
        compiler_params=_cparams(("arbitrary", "arbitrary")),
        name="dn_intra",
    )(q, k, v, gb, gbt)


def _dn_state_body(*refs, rev):
    if rev:
        uc_ref, wc_ref, qe_ref, kdt_ref, qk_ref, gb_ref, of_ref, o_ref, s_scr = refs
    else:
        uc_ref, wc_ref, qe_ref, kdt_ref, qk_ref, gb_ref, o_ref, s_scr = refs
    d = 1 if rev else 0
    last = 0 if rev else DN_CHUNK - 1

    @pl.when(pl.program_id(1) == 0)
    def _():
        s_scr[...] = jnp.zeros_like(s_scr)

    for hh in range(DN_HEADS):
        hs = slice(hh * DN_DK, (hh + 1) * DN_DK)
        gl = 2 * DN_HEADS + d * DN_HEADS + hh
        s = s_scr[hh]
        sb = s.astype(bf16)
        v_new = uc_ref[0, 0, :, hs] - jnp.dot(wc_ref[0, 0, :, hs], sb, preferred_element_type=f32)
        vb = v_new.astype(bf16)
        o_h = (jnp.dot(qe_ref[0, 0, :, hs], sb, preferred_element_type=f32)
               + jnp.dot(qk_ref[0, 0, 0, hh], vb, preferred_element_type=f32))
        eg = jnp.exp(gb_ref[0, last:last + 1, gl:gl + 1])
        s_scr[hh] = s * eg + jnp.dot(kdt_ref[0, 0, 0, hs, :], vb, preferred_element_type=f32)
        if rev:
            o_h = o_h + of_ref[0, :, hs]
        o_ref[0, :, hs] = o_h


def _dn_state(uc, wc, qe, kdt, qkm, gb, n_ctx_chunks, rev, o_fwd=None):
    _, B, Lt, W = uc.shape
    nc = Lt // DN_CHUNK
    d = 1 if rev else 0
    if rev:
        cidx = lambda i: jnp.where(i < n_ctx_chunks, n_ctx_chunks - 1 - i, nc - 1 - (i - n_ctx_chunks))
    else:
        cidx = lambda i: i
    tok = lambda w: pl.BlockSpec((1, DN_CHUNK, w), lambda b, i: (b, cidx(i), 0))
    tokd = lambda w: pl.BlockSpec((1, 1, DN_CHUNK, w), lambda b, i: (d, b, cidx(i), 0))
    in_specs = [tokd(W), tokd(W), tokd(W),
                pl.BlockSpec((1, 1, 1, W, DN_CHUNK), lambda b, i: (d, b, cidx(i), 0, 0)),
                pl.BlockSpec((1, 1, 1, DN_HEADS, DN_CHUNK, DN_CHUNK), lambda b, i: (d, b, cidx(i), 0, 0, 0)),
                tok(LANES)]
    args = [uc, wc, qe, kdt, qkm, gb]
    if rev:
        in_specs.append(tok(W))
        args.append(o_fwd)
    return pl.pallas_call(
        functools.partial(_dn_state_body, rev=rev),
        out_shape=jax.ShapeDtypeStruct((B, Lt, W), f32),
        grid=(B, nc),
        in_specs=in_specs,
        out_specs=tok(W),
        scratch_shapes=[pltpu.VMEM((DN_HEADS, DN_DK, DN_DV), f32)],
        compiler_params=_cparams(("arbitrary", "arbitrary")),
        name="dn_state_bwd" if rev else "dn_state_fwd",
    )(*args)


def _attn_body(q_ref, kp_ref, kc_ref, kn_ref, kx_ref, vp_ref, vc_ref, vn_ref, vx_ref, sink_ref, o_ref,
               *, n_ctx_blocks, seq_len):
    Tq = AT_BLOCK
    G = AT_HEADS // AT_KV
    i = pl.program_id(1)
    is_ctx = i < n_ctx_blocks
    nloc = 3 * Tq
    qpos = (i - n_ctx_blocks) * Tq + lax.broadcasted_iota(jnp.int32, (Tq, nloc), 0)
    kpos = (i - n_ctx_blocks - 1) * Tq + lax.broadcasted_iota(jnp.int32, (Tq, nloc), 1)
    ok = jnp.logical_and(jnp.abs(qpos - kpos) <= WINDOW, jnp.logical_and(kpos >= 0, kpos < seq_len))
    ok = jnp.logical_and(ok, jnp.logical_not(is_ctx))
    lctx = kx_ref.shape[1]
    valid = jnp.concatenate([ok, jnp.ones((Tq, lctx), dtype=jnp.bool_)], axis=1)
    valid = jnp.concatenate([valid] * G, axis=0)
    scale = AT_HD ** -0.5
    for hk in range(AT_KV):
        ks = slice(hk * AT_HD, (hk + 1) * AT_HD)
        kall = jnp.concatenate([kp_ref[0, :, ks], kc_ref[0, :, ks], kn_ref[0, :, ks], kx_ref[0, :, ks]], axis=0)
        vall = jnp.concatenate([vp_ref[0, :, ks], vc_ref[0, :, ks], vn_ref[0, :, ks], vx_ref[0, :, ks]], axis=0)
        qs = jnp.concatenate([q_ref[0, :, (hk * G + g) * AT_HD:(hk * G + g + 1) * AT_HD] for g in range(G)],
                             axis=0)
        sink = jnp.concatenate([jnp.broadcast_to(sink_ref[0:1, hk * G + g:hk * G + g + 1], (Tq, 1))
                                for g in range(G)], axis=0)
        s = jnp.where(valid, _dot_nt(qs, kall) * scale, -jnp.inf)
        m = jnp.maximum(jnp.max(s, axis=-1, keepdims=True), sink)
        p = jnp.exp(s - m)
        denom = jnp.sum(p, axis=-1, keepdims=True) + jnp.exp(sink - m)
        o = _dot(p, vall) / denom
        for g in range(G):
            hq = hk * G + g
            o_ref[0, :, hq * AT_HD:(hq + 1) * AT_HD] = o[g * Tq:(g + 1) * Tq, :]


def _attention(aq, ak, av, sink_vec, n_ctx, seq_len):
    B, Lt, _ = aq.shape
    nb = Lt // AT_BLOCK
    n_ctx_blocks = n_ctx // AT_BLOCK
    blk = lambda w, f: pl.BlockSpec((1, AT_BLOCK, w), lambda b, i: (b, f(i), 0))
    lo, hi = n_ctx_blocks, nb - 1
    prev = lambda i: jnp.clip(i - 1, lo, hi)
    cur = lambda i: i
    nxt = lambda i: jnp.clip(i + 1, lo, hi)
    ctx_spec = pl.BlockSpec((1, n_ctx, AT_KW), lambda b, i: (b, 0, 0))
    body = functools.partial(_attn_body, n_ctx_blocks=n_ctx_blocks, seq_len=seq_len)
    return pl.pallas_call(
        body,
        out_shape=jax.ShapeDtypeStruct((B, Lt, AT_QW), f32),
        grid=(B, nb),
        in_specs=[blk(AT_QW, cur),
                  blk(AT_KW, prev), blk(AT_KW, cur), blk(AT_KW, nxt), ctx_spec,
                  blk(AT_KW, prev), blk(AT_KW, cur), blk(AT_KW, nxt), ctx_spec,
                  _const_spec((1, LANES))],
        out_specs=blk(AT_QW, cur),
        compiler_params=_cparams(("arbitrary", "arbitrary")),
        name="window_attn",
    )(aq, ak, ak, ak, ak, av, av, av, av, sink_vec)


def _merge_body(x_ref, mod_ref, g_ref, ys_ref, od_ref, z_ref, yc_ref,
                wg_ref, wglu_ref, ng_ref, wa_ref, wb_ref, wc_ref, wo_ref, o_ref):
    x = x_ref[0]
    h = _modnorm(x, g_ref[...], mod_ref[0, 1:2, :], mod_ref[0, 0:1, :]).astype(bf16)
    zs = jax.nn.gelu(ys_ref[...])
    ya = zs * jax.nn.sigmoid(_dot(zs, wglu_ref[...]))
    ng = ng_ref[...]
    parts = []
    for hh in range(DN_HEADS):
        hs = slice(hh * DN_DV, (hh + 1) * DN_DV)
        o = od_ref[0, :, hs]
        on = o * lax.rsqrt(jnp.mean(o * o, axis=-1, keepdims=True) + EPS) * ng
        parts.append(on * _silu(z_ref[0, :, hs]))
    yb = jnp.concatenate(parts, axis=1)
    yc = yc_ref[0]
    D = D_MODEL

    def gate(j):
        return jax.nn.sigmoid(jnp.dot(h, wg_ref[:, j * D:(j + 1) * D], preferred_element_type=f32))

    m = gate(0) * _dot(ya, wa_ref[...])
    m = m + gate(1) * _dot(yb, wb_ref[...])
    m = m + gate(2) * _dot(yc, wc_ref[...])
    mix = _dot(m, wo_ref[...])
    o_ref[0] = x + mod_ref[0, 2:3, :] * mix


def _merge(xcat, mods, g1, ys_tm, o_dn, dz, yc, w_gate, w_glu, dn_g, w_a, w_b, w_c, w_o, n_ctx_tiles):
    B, Lt, D = xcat.shape
    nt = Lt // TOK_TILE
    tok = lambda w: pl.BlockSpec((1, TOK_TILE, w), lambda b, t: (b, t, 0))
    return pl.pallas_call(
        _merge_body,
        out_shape=jax.ShapeDtypeStruct((B, Lt, D), f32),
        grid=(B, nt),
        in_specs=[tok(D),
                  pl.BlockSpec((1, N_MOD, D), _mod_index(n_ctx_tiles, B)),
                  _const_spec((1, D)),
                  pl.BlockSpec((TOK_TILE, SSM_WIDTH), lambda b, t: (t, b)),
                  tok(DN_W), tok(DN_W), tok(AT_QW),
                  _const_spec(w_gate.shape), _const_spec(w_glu.shape), _const_spec((1, DN_DV)),
                  _const_spec(w_a.shape), _const_spec(w_b.shape), _const_spec(w_c.shape),
                  _const_spec(w_o.shape)],
        out_specs=tok(D),
        compiler_params=_cparams(("arbitrary", "arbitrary")),
        name="merge_out",
    )(xcat, mods, g1, ys_tm, o_dn, dz, yc, w_gate, w_glu, dn_g, w_a, w_b, w_c, w_o)


FF_CHUNK = 1024


def _ffn_body(x_ref, mod_ref, g_ref, w1_ref, w2_ref, o_ref):
    x = x_ref[0]
    h = _modnorm(x, g_ref[...], mod_ref[0, 4:5, :], mod_ref[0, 3:4, :]).astype(bf16)
    acc = jnp.zeros(x.shape, f32)
    for j in range(D_FF // FF_CHUNK):
        a = jnp.dot(h, w1_ref[:, j * FF_CHUNK:(j + 1) * FF_CHUNK], preferred_element_type=f32)
        a = jnp.square(jnp.maximum(a, 0.0)).astype(bf16)
        acc = acc + jnp.dot(a, w2_ref[j * FF_CHUNK:(j + 1) * FF_CHUNK, :], preferred_element_type=f32)
    o_ref[0] = x + mod_ref[0, 5:6, :] * acc


def _ffn(xcat, mods, g2, w1, w2, n_ctx_tiles):
    B, Lt, D = xcat.shape
    nt = Lt // TOK_TILE
    tok = pl.BlockSpec((1, TOK_TILE, D), lambda b, t: (b, t, 0))
    return pl.pallas_call(
        _ffn_body,
        out_shape=jax.ShapeDtypeStruct((B, Lt, D), f32),
        grid=(B, nt),
        in_specs=[tok, pl.BlockSpec((1, N_MOD, D), _mod_index(n_ctx_tiles, B)), _const_spec((1, D)),
                  _const_spec(w1.shape), _const_spec(w2.shape)],
        out_specs=tok,
        compiler_params=_cparams(("arbitrary", "arbitrary")),
        name="ffn",
    )(xcat, mods, g2, w1, w2)


def _final_body(x_ref, g_ref, o_ref):
    x = x_ref[0]
    o_ref[0] = x * lax.rsqrt(jnp.mean(x * x, axis=-1, keepdims=True) + EPS) * g_ref[...]


def _final_norm(xcat, g, n_ctx_tiles, seq_len):
    B, Lt, D = xcat.shape
    return pl.pallas_call(
        _final_body,
        out_shape=jax.ShapeDtypeStruct((B, seq_len, D), f32),
        grid=(B, seq_len // TOK_TILE),
        in_specs=[pl.BlockSpec((1, TOK_TILE, D), lambda b, t: (b, t + n_ctx_tiles, 0)), _const_spec((1, D))],
        out_specs=pl.BlockSpec((1, TOK_TILE, D), lambda b, t: (b, t, 0)),
        compiler_params=_cparams(("arbitrary", "arbitrary")),
        name="final_norm",
    )(xcat, g)


def _rope_tables(n_ctx, seq_len):
    n = AT_HD // 4
    inv_freq = ROPE_BASE ** (-jnp.arange(n, dtype=f32) / n)
    pos = jnp.arange(seq_len, dtype=jnp.int32)
    rows = (pos // GRID_W).astype(f32)[:, None] * inv_freq[None, :]
    cols = (pos % GRID_W).astype(f32)[:, None] * inv_freq[None, :]
    cos = jnp.concatenate([jnp.cos(rows)] * 2 + [jnp.cos(cols)] * 2, axis=1)
    sin = jnp.concatenate([-jnp.sin(rows), jnp.sin(rows), -jnp.sin(cols), jnp.sin(cols)], axis=1)
    cos = jnp.concatenate([jnp.ones((n_ctx, AT_HD), f32), cos], axis=0)
    sin = jnp.concatenate([jnp.zeros((n_ctx, AT_HD), f32), sin], axis=0)
    return jnp.tile(cos, (1, LANES // AT_HD)), jnp.tile(sin, (1, LANES // AT_HD))


def _split_w_in(w_in):
    o = SSM_WIDTH + 4 * DN_W
    w_ba = w_in[:, o:o + 4 * DN_HEADS]
    o2 = o + 4 * DN_HEADS
    w_att = w_in[:, o2:o2 + AT_QW + 2 * AT_KW]
    w_gate = w_in[:, o2 + AT_QW + 2 * AT_KW:]
    w_main = jnp.concatenate([w_in[:, :o], w_att], axis=1).astype(bf16)
    w_ba = jnp.pad(w_ba, ((0, 0), (0, LANES - 4 * DN_HEADS))).astype(bf16)
    return w_main, w_ba, w_gate.astype(bf16)


def kernel(x, c, ctx, c_ctx, norm1_g, norm2_g, w_mod, b_mod, w_in, ssm_lam_re, ssm_lam_im, ssm_log_dt, ssm_b_re, ssm_b_im, ssm_c_re, ssm_c_im, ssm_d, ssm_w_glu, dn_conv_w, dn_a_log, dn_dt_bias, dn_norm_g, attn_sink, w_branch_a, w_branch_b, w_branch_c, w_out, w_ff1, w_ff2, final_norm_g):
    B, L, D = x.shape
    Lc = ctx.shape[1]
    depth = w_in.shape[0]
    assert B == SUBLANES and D == D_MODEL
    assert Lc % TOK_TILE == 0 and L % TOK_TILE == 0
    Lt = Lc + L
    n_ctx_tiles = Lc // TOK_TILE

    cond = jnp.concatenate([c, c_ctx[None, :], jnp.zeros((16 - B - 1, D), f32)], axis=0)
    mods_all = _modulation(cond, w_mod, b_mod).reshape(depth, 16, N_MOD, D)
    cos_t, sin_t = _rope_tables(Lc, L)
    xcat = jnp.concatenate([ctx, x], axis=1)

    for layer in range(depth):
        mods = mods_all[layer]
        g1 = norm1_g[layer].reshape(1, D)
        g2 = norm2_g[layer].reshape(1, D)
        w_main, w_ba, w_gate = _split_w_in(w_in[layer])

        u_tm, dqkv, dz, dba, aq, ak, av = _inproj(xcat, mods, g1, w_main, w_ba, cos_t, sin_t, n_ctx_tiles)

        a_re, a_im, bbt_re, bbt_im = _s5_discretize(ssm_lam_re[layer], ssm_lam_im[layer], ssm_log_dt[layer],
                                                    ssm_b_re[layer], ssm_b_im[layer])
        bmat, cmat, avec = _s5_matrices(a_re, a_im, bbt_re, bbt_im, ssm_c_re[layer], ssm_c_im[layer])
        u_rows = u_tm.reshape(Lt * B, SSM_WIDTH)
        y_f = _s5_scan(u_rows, bmat, cmat, avec, Lc, rev=False)
        y_s5 = _s5_scan(u_rows, bmat, cmat, avec, Lc, rev=True, yf=y_f,
                        d_skip=ssm_d[layer].reshape(1, SSM_WIDTH))
        ys_tm = y_s5.reshape(Lt, B * SSM_WIDTH)

        conv_w = jnp.pad(dn_conv_w[layer], ((0, SUBLANES - DN_CONV), (0, 0)))
        pad_l = 2 * DN_HEADS
        al_vec = jnp.pad(dn_a_log[layer].reshape(1, 2 * DN_HEADS), ((0, 0), (pad_l, LANES - 2 * pad_l)))
        dtb_vec = jnp.pad(dn_dt_bias[layer].reshape(1, 2 * DN_HEADS), ((0, 0), (pad_l, LANES - 2 * pad_l)))
        qn, kn, vs, gb, gbt = _dn_prepare(dqkv, dba, conv_w, al_vec, dtb_vec, n_ctx_tiles)
        n_ctx_chunks = Lc // DN_CHUNK
        uc, wc, qe, kdt, qkm = _dn_intra(qn, kn, vs, gb, gbt)
        o_f = _dn_state(uc, wc, qe, kdt, qkm, gb, n_ctx_chunks, rev=False)
        o_dn = _dn_state(uc, wc, qe, kdt, qkm, gb, n_ctx_chunks, rev=True, o_fwd=o_f)

        sink_vec = jnp.pad(attn_sink[layer].reshape(1, AT_HEADS), ((0, 0), (0, LANES - AT_HEADS)))
        yc = _attention(aq, ak, av, sink_vec, Lc, L)

        x1 = _merge(xcat, mods, g1, ys_tm, o_dn, dz, yc, w_gate, ssm_w_glu[layer].astype(bf16),
                    dn_norm_g[layer].reshape(1, DN_DV), w_branch_a[layer].astype(bf16),
                    w_branch_b[layer].astype(bf16), w_branch_c[layer].astype(bf16),
                    w_out[layer].astype(bf16), n_ctx_tiles)
        xcat = _ffn(x1, mods, g2, w_ff1[layer].astype(bf16), w_ff2[layer].astype(bf16), n_ctx_tiles)

    return _final_norm(xcat, final_norm_g.reshape(1, D), n_ctx_tiles, L)
```

```python
import functools
import math

import jax
import jax.numpy as jnp
from jax import lax
from jax.experimental import pallas as pl
from jax.experimental.pallas import tpu as pltpu

f32 = jnp.float32
bf16 = jnp.bfloat16
HI = lax.Precision.HIGHEST

D_MODEL = 1024
GRID_W = 64
EPS = 1e-6
SSM_WIDTH = D_MODEL // 2
SSM_GROUP = 16
SSM_GROUPS = SSM_WIDTH // SSM_GROUP
SSM_STATE = 64
DN_HEADS = 4
DN_DK = 128
DN_DV = 128
DN_CONV = 5
DN_CHUNK = 64
AT_HEADS = 8
AT_KV = 2
AT_HD = 64
WINDOW = 128
AT_BLOCK = 128
ROPE_BASE = 10000.0
D_FF = 4 * D_MODEL
N_MOD = 6
DN_W = DN_HEADS * DN_DK
AT_QW = AT_HEADS * AT_HD
AT_KW = AT_KV * AT_HD

SUBLANES = 8
LANES = 128
TOK_TILE = 256
S5_STEPS = 32
S5_COLBLK = 128
S5_NBLK = SSM_WIDTH // S5_COLBLK
S5_SBLK = (S5_COLBLK // SSM_GROUP) * SSM_STATE
VMEM_LIMIT = 56 * 1024 * 1024


def _cparams(sem):
    return pltpu.CompilerParams(dimension_semantics=sem, vmem_limit_bytes=VMEM_LIMIT)


def _const_spec(shape):
    nd = len(shape)
    return pl.BlockSpec(shape, lambda *_: (0,) * nd, pipeline_mode=pl.Buffered(1))


def _dot(a, b):
    return jnp.dot(a.astype(bf16), b.astype(bf16), preferred_element_type=f32)


def _dot_nt(a, b):
    return lax.dot_general(a.astype(bf16), b.astype(bf16), (((1,), (1,)), ((), ())),
                           preferred_element_type=f32)


def _dot_tn(a, b):
    return lax.dot_general(a.astype(bf16), b.astype(bf16), (((0,), (0,)), ((), ())),
                           preferred_element_type=f32)


def _modnorm(x, g, scale, shift):
    y = x * lax.rsqrt(jnp.mean(x * x, axis=-1, keepdims=True) + EPS)
    return (y * g) * (1.0 + scale) + shift


def _silu(x):
    return x * jax.nn.sigmoid(x)


def _mod_body(c_ref, w_ref, b_ref, o_ref):
    s = _silu(c_ref[...])
    o_ref[0] = jnp.dot(s, w_ref[0], precision=HI, preferred_element_type=f32) + b_ref[0]


def _modulation(cond, w_mod, b_mod):
    depth = w_mod.shape[0]
    nblk = (N_MOD * D_MODEL) // D_MODEL
    return pl.pallas_call(
        _mod_body,
        out_shape=jax.ShapeDtypeStruct((depth, 16, N_MOD * D_MODEL), f32),
        grid=(depth, nblk),
        in_specs=[pl.BlockSpec((16, D_MODEL), lambda l, j: (0, 0)),
                  pl.BlockSpec((1, D_MODEL, D_MODEL), lambda l, j: (l, 0, j)),
                  pl.BlockSpec((1, 1, D_MODEL), lambda l, j: (l, 0, j))],
        out_specs=pl.BlockSpec((1, 16, D_MODEL), lambda l, j: (l, 0, j)),
        compiler_params=_cparams(("arbitrary", "arbitrary")),
        name="adaln_mod",
    )(cond, w_mod, b_mod.reshape(depth, 1, N_MOD * D_MODEL))


def _rope(x, cos, sin):
    n = x.shape[-1]
    lane = lax.broadcasted_iota(jnp.int32, x.shape, 1)
    partner = jnp.where((lane % 32) < 16, pltpu.roll(x, n - 16, 1), pltpu.roll(x, 16, 1))
    return x * cos + partner * sin


def _inproj_body(x_ref, mod_ref, g_ref, wm_ref, wba_ref, cos_ref, sin_ref,
                 u_ref, qkv_ref, z_ref, ba_ref, aq_ref, ak_ref, av_ref):
    h = _modnorm(x_ref[0], g_ref[...], mod_ref[0, 1:2, :], mod_ref[0, 0:1, :]).astype(bf16)

    def proj(lo, hi):
        return jnp.dot(h, wm_ref[:, lo:hi], preferred_element_type=f32)

    o = 0
    u_ref[...] = proj(o, o + SSM_WIDTH)
    o += SSM_WIDTH
    qkv_ref[0] = proj(o, o + 3 * DN_W)
    o += 3 * DN_W
    z_ref[0] = proj(o, o + DN_W)
    o += DN_W
    cos = cos_ref[...]
    sin = sin_ref[...]
    reps = AT_QW // LANES
    aq_ref[0] = _rope(proj(o, o + AT_QW), jnp.concatenate([cos] * reps, axis=1),
                      jnp.concatenate([sin] * reps, axis=1))
    o += AT_QW
    ak_ref[0] = _rope(proj(o, o + AT_KW), cos, sin)
    o += AT_KW
    av_ref[0] = proj(o, o + AT_KW)
    ba_ref[0] = jnp.dot(h, wba_ref[...], preferred_element_type=f32)


def _mod_index(n_ctx_tiles, ctx_row):
    return lambda b, t: (jnp.where(t < n_ctx_tiles, ctx_row, b), 0, 0)


def _inproj(xcat, mods, g1, w_main, w_ba, cos_t, sin_t, n_ctx_tiles):
    B, Lt, D = xcat.shape
    nt = Lt // TOK_TILE
    wm = w_main.shape[1]
    tok = lambda w: pl.BlockSpec((1, TOK_TILE, w), lambda b, t: (b, t, 0))
    out_shape = (
        jax.ShapeDtypeStruct((Lt, B * SSM_WIDTH), f32),
        jax.ShapeDtypeStruct((B, Lt, 3 * DN_W), f32),
        jax.ShapeDtypeStruct((B, Lt, DN_W), f32),
        jax.ShapeDtypeStruct((B, Lt, LANES), f32),
        jax.ShapeDtypeStruct((B, Lt, AT_QW), f32),
        jax.ShapeDtypeStruct((B, Lt, AT_KW), f32),
        jax.ShapeDtypeStruct((B, Lt, AT_KW), f32),
    )
    return pl.pallas_call(
        _inproj_body,
        out_shape=out_shape,
        grid=(B, nt),
        in_specs=[tok(D),
                  pl.BlockSpec((1, N_MOD, D), _mod_index(n_ctx_tiles, B)),
                  _const_spec((1, D)),
                  _const_spec((D, wm)),
                  _const_spec((D, LANES)),
                  pl.BlockSpec((TOK_TILE, LANES), lambda b, t: (t, 0)),
                  pl.BlockSpec((TOK_TILE, LANES), lambda b, t: (t, 0))],
        out_specs=(pl.BlockSpec((TOK_TILE, SSM_WIDTH), lambda b, t: (t, b)),
                   tok(3 * DN_W), tok(DN_W), tok(LANES), tok(AT_QW), tok(AT_KW), tok(AT_KW)),
        compiler_params=_cparams(("arbitrary", "arbitrary")),
        name="in_proj",
    )(xcat, mods, g1, w_main, w_ba, cos_t, sin_t)


def _s5disc_body(lre_ref, lim_ref, ldt_ref, bre_ref, bim_ref, are_ref, aim_ref, bbre_ref, bbim_ref):
    lr = lre_ref[0]
    li = lim_ref[0]
    dt = jnp.exp(ldt_ref[0])
    mag = jnp.exp(lr * dt)
    a_re = mag * jnp.cos(li * dt)
    a_im = mag * jnp.sin(li * dt)
    den = lr * lr + li * li
    f_re = ((a_re - 1.0) * lr + a_im * li) / den
    f_im = (a_im * lr - (a_re - 1.0) * li) / den
    are_ref[0] = a_re
    aim_ref[0] = a_im
    b_re = bre_ref[0]
    b_im = bim_ref[0]
    bbre_ref[0] = f_re * b_re - f_im * b_im
    bbim_ref[0] = f_re * b_im + f_im * b_re


def _s5_discretize(lam_re, lam_im, log_dt, b_re, b_im):
    n = 2 * SSM_GROUPS
    P, H = SSM_STATE, SSM_GROUP
    vec = pl.BlockSpec((1, 1, P), lambda i: (i, 0, 0))
    mat = pl.BlockSpec((1, H, P), lambda i: (i, 0, 0))
    return pl.pallas_call(
        _s5disc_body,
        out_shape=(jax.ShapeDtypeStruct((n, 1, P), f32), jax.ShapeDtypeStruct((n, 1, P), f32),
                   jax.ShapeDtypeStruct((n, H, P), f32), jax.ShapeDtypeStruct((n, H, P), f32)),
        grid=(n,),
        in_specs=[vec, vec, pl.BlockSpec((1, 1, 1), lambda i: (i, 0, 0)), mat, mat],
        out_specs=(vec, vec, mat, mat),
        compiler_params=_cparams(("arbitrary",)),
        name="s5_discretize",
    )(lam_re.reshape(n, 1, P), lam_im.reshape(n, 1, P), log_dt.reshape(n, 1, 1),
      jnp.swapaxes(b_re, -1, -2).reshape(n, H, P), jnp.swapaxes(b_im, -1, -2).reshape(n, H, P))


def _s5_matrices(a_re, a_im, bbt_re, bbt_im, c_re, c_im):
    gb = S5_COLBLK // SSM_GROUP
    P, H = SSM_STATE, SSM_GROUP
    eye = jnp.eye(gb, dtype=f32)

    def bblk(t):
        t = t.reshape(2, S5_NBLK, gb, H, P)
        return jnp.einsum('dcghp,gk->dcghkp', t, eye).reshape(2, S5_NBLK, gb * H, gb * P)

    def cblk(t):
        t = t.reshape(2, S5_NBLK, gb, H, P)
        return jnp.einsum('dcghp,gk->dcgpkh', t, eye).reshape(2, S5_NBLK, gb * P, gb * H)

    bmat = jnp.concatenate([bblk(bbt_re), bblk(bbt_im)], axis=-1).astype(bf16)
    cmat = jnp.concatenate([cblk(c_re.astype(f32)), -cblk(c_im.astype(f32))], axis=-2).astype(bf16)
    avec = jnp.stack([a_re.reshape(2, S5_NBLK, gb * P), a_im.reshape(2, S5_NBLK, gb * P)], axis=2)
    return bmat, cmat, avec


def _s5_scan_body(*refs, rev):
    if rev:
        u_ref, bm_ref, cm_ref, a_ref, yf_ref, d_ref, y_ref, bu_scr, st_scr = refs
    else:
        u_ref, bm_ref, cm_ref, a_ref, y_ref, bu_scr, st_scr = refs
    B = SUBLANES
    nsb = S5_SBLK

    @pl.when(pl.program_id(0) == 0)
    def _():
        st_scr[...] = jnp.zeros_like(st_scr)

    for cb in range(S5_NBLK):
        cols = slice(cb * S5_COLBLK, (cb + 1) * S5_COLBLK)
        bu_scr[...] = jnp.dot(u_ref[:, cols].astype(bf16), bm_ref[0, cb], preferred_element_type=f32)
        are = jnp.broadcast_to(a_ref[0, cb, 0:1, :], (B, nsb))
        aim = jnp.broadcast_to(a_ref[0, cb, 1:2, :], (B, nsb))

        def step(i, carry, are=are, aim=aim):
            sre, sim = carry
            t = (S5_STEPS - 1 - i) if rev else i
            r0 = pl.multiple_of(t * B, B)
            nre = are * sre - aim * sim + bu_scr[pl.ds(r0, B), 0:nsb]
            nim = are * sim + aim * sre + bu_scr[pl.ds(r0, B), nsb:2 * nsb]
            bu_scr[pl.ds(r0, B), 0:nsb] = nre
            bu_scr[pl.ds(r0, B), nsb:2 * nsb] = nim
            return nre, nim

        sre, sim = lax.fori_loop(0, S5_STEPS, step, (st_scr[cb, :, 0:nsb], st_scr[cb, :, nsb:2 * nsb]),
                                 unroll=True)
        st_scr[cb, :, 0:nsb] = sre
        st_scr[cb, :, nsb:2 * nsb] = sim
        y = jnp.dot(bu_scr[...].astype(bf16), cm_ref[0, cb], preferred_element_type=f32)
        if rev:
            y = y + yf_ref[:, cols] + d_ref[:, cols] * u_ref[:, cols]
        y_ref[:, cols] = y


def _s5_tile_index(n_ctx_tiles, n_tiles, rev):
    if not rev:
        return lambda i: (i, 0)
    return lambda i: (jnp.where(i < n_ctx_tiles, n_ctx_tiles - 1 - i, n_tiles - 1 - (i - n_ctx_tiles)), 0)


def _s5_scan(u_rows, bmat, cmat, avec, n_ctx_steps, rev, yf=None, d_skip=None):
    n_rows = u_rows.shape[0]
    rows = S5_STEPS * SUBLANES
    n_tiles = n_rows // rows
    n_ctx_tiles = n_ctx_steps // S5_STEPS
    idx = _s5_tile_index(n_ctx_tiles, n_tiles, rev)
    d = 1 if rev else 0
    tile = pl.BlockSpec((rows, SSM_WIDTH), idx)
    in_specs = [tile,
                pl.BlockSpec((1, S5_NBLK, S5_COLBLK, 2 * S5_SBLK), lambda i: (d, 0, 0, 0), pipeline_mode=pl.Buffered(1)),
                pl.BlockSpec((1, S5_NBLK, 2 * S5_SBLK, S5_COLBLK), lambda i: (d, 0, 0, 0), pipeline_mode=pl.Buffered(1)),
                pl.BlockSpec((1, S5_NBLK, 2, S5_SBLK), lambda i: (d, 0, 0, 0), pipeline_mode=pl.Buffered(1))]
    args = [u_rows, bmat, cmat, avec]
    if rev:
        in_specs += [tile, _const_spec((1, SSM_WIDTH))]
        args += [yf, d_skip]
    return pl.pallas_call(
        functools.partial(_s5_scan_body, rev=rev),
        out_shape=jax.ShapeDtypeStruct((n_rows, SSM_WIDTH), f32),
        grid=(n_tiles,),
        in_specs=in_specs,
        out_specs=tile,
        scratch_shapes=[pltpu.VMEM((rows, 2 * S5_SBLK), f32),
                        pltpu.VMEM((S5_NBLK, SUBLANES, 2 * S5_SBLK), f32)],
        compiler_params=_cparams(("arbitrary",)),
        name="s5_scan_bwd" if rev else "s5_scan_fwd",
    )(*args)


def _dn_prep_body(qkv_ref, prev_ref, next_ref, ba_ref, cw_ref, al_ref, dtb_ref,
                  q_ref, k_ref, v_ref, gb_ref, gbt_ref, ext_scr, *, n_ctx_tiles, n_tiles):
    T = TOK_TILE
    t = pl.program_id(1)
    prev_ok = jnp.logical_and(t != 0, t != n_ctx_tiles)
    next_ok = jnp.logical_and(t != n_ctx_tiles - 1, t != n_tiles - 1)
    ext_scr[0:SUBLANES, :] = jnp.where(prev_ok, prev_ref[0], 0.0)
    ext_scr[SUBLANES:SUBLANES + T, :] = qkv_ref[0]
    ext_scr[SUBLANES + T:2 * SUBLANES + T, :] = jnp.where(next_ok, next_ref[0], 0.0)
    half = DN_CONV // 2
    acc = cw_ref[0:1, :] * ext_scr[pl.ds(SUBLANES - half, T), :]
    for kk in range(1, DN_CONV):
        acc = acc + cw_ref[kk:kk + 1, :] * ext_scr[pl.ds(SUBLANES - half + kk, T), :]
    act = _silu(acc)
    for hh in range(DN_HEADS):
        qs = act[:, hh * DN_DK:(hh + 1) * DN_DK]
        ks = act[:, DN_W + hh * DN_DK:DN_W + (hh + 1) * DN_DK]
        q_ref[0, :, hh * DN_DK:(hh + 1) * DN_DK] = qs * (lax.rsqrt(jnp.sum(qs * qs, axis=-1, keepdims=True) + EPS)
                                                       * (DN_DK ** -0.5))
        k_ref[0, :, hh * DN_DK:(hh + 1) * DN_DK] = ks * lax.rsqrt(jnp.sum(ks * ks, axis=-1, keepdims=True) + EPS)
    v_ref[0] = act[:, 2 * DN_W:3 * DN_W]

    ba = ba_ref[0]
    lane = lax.broadcasted_iota(jnp.int32, ba.shape, 1)
    beta = jax.nn.sigmoid(ba)
    g = -jnp.exp(al_ref[...]) * jax.nn.softplus(ba + dtb_ref[...])
    ri = lax.broadcasted_iota(jnp.int32, (T, T), 0)
    ci = lax.broadcasted_iota(jnp.int32, (T, T), 1)
    same = (ri // DN_CHUNK) == (ci // DN_CHUNK)
    tri_f = jnp.logical_and(same, ri >= ci).astype(f32)
    tri_b = jnp.logical_and(same, ri <= ci).astype(f32)
    gc_f = jnp.dot(tri_f, g, precision=HI, preferred_element_type=f32)
    gc_b = jnp.dot(tri_b, g, precision=HI, preferred_element_type=f32)
    nh = DN_HEADS
    gc = jnp.where(lane < 2 * nh + nh, gc_f, gc_b)
    gb = jnp.where(lane < 2 * nh, beta, jnp.where(lane < 4 * nh, gc, 0.0))
    gb_ref[0] = gb
    er = lax.broadcasted_iota(jnp.int32, (16, LANES), 0)
    ec = lax.broadcasted_iota(jnp.int32, (16, LANES), 1)
    sel = (er == ec).astype(f32)
    gbt = lax.dot_general(sel, gb, (((1,), (1,)), ((), ())), precision=HI, preferred_element_type=f32)
    for c in range(T // DN_CHUNK):
        gbt_ref[0, c] = gbt[:, c * DN_CHUNK:(c + 1) * DN_CHUNK]


def _dn_prepare(qkv, ba, conv_w, al_vec, dtb_vec, n_ctx_tiles):
    B, Lt, W = qkv.shape
    nt = Lt // TOK_TILE
    per8 = TOK_TILE // SUBLANES
    nblk8 = Lt // SUBLANES
    tok = lambda w: pl.BlockSpec((1, TOK_TILE, w), lambda b, t: (b, t, 0))
    body = functools.partial(_dn_prep_body, n_ctx_tiles=n_ctx_tiles, n_tiles=nt)
    return pl.pallas_call(
        body,
        out_shape=(jax.ShapeDtypeStruct((B, Lt, DN_W), f32),
                   jax.ShapeDtypeStruct((B, Lt, DN_W), f32),
                   jax.ShapeDtypeStruct((B, Lt, DN_W), f32),
                   jax.ShapeDtypeStruct((B, Lt, LANES), f32),
                   jax.ShapeDtypeStruct((B, Lt // DN_CHUNK, 16, DN_CHUNK), f32)),
        grid=(B, nt),
        in_specs=[tok(W),
                  pl.BlockSpec((1, SUBLANES, W), lambda b, t: (b, jnp.maximum(t * per8 - 1, 0), 0)),
                  pl.BlockSpec((1, SUBLANES, W), lambda b, t: (b, jnp.minimum((t + 1) * per8, nblk8 - 1), 0)),
                  tok(LANES),
                  _const_spec((SUBLANES, W)),
                  _const_spec((1, LANES)),
                  _const_spec((1, LANES))],
        out_specs=(tok(DN_W), tok(DN_W), tok(DN_W), tok(LANES),
                   pl.BlockSpec((1, TOK_TILE // DN_CHUNK, 16, DN_CHUNK), lambda b, t: (b, t, 0, 0))),
        scratch_shapes=[pltpu.VMEM((TOK_TILE + 2 * SUBLANES, W), f32)],
        compiler_params=_cparams(("arbitrary", "arbitrary")),
        name="dn_prepare",
    )(qkv, qkv, qkv, ba, conv_w, al_vec, dtb_vec)


DN_HALF = DN_CHUNK // 2
DN_PAIRS = DN_HEADS // 2
DN_STATE_BATCH = 2


def _diag_block_inverses(n_scr, revs):
    H = DN_HALF
    ng = H // SUBLANES
    ri = lax.broadcasted_iota(jnp.int32, (SUBLANES, LANES), 0)
    ci = lax.broadcasted_iota(jnp.int32, (SUBLANES, LANES), 1)
    eye = [(ri + SUBLANES * g == ci % H).astype(f32) for g in range(ng)]
    base = (ci // H) * H
    ts = [list(eye) for _ in revs]
    for step in range(H - 1):
        for p, rev in enumerate(revs):
            j = H - 1 - step if rev else step
            gj = j // SUBLANES
            idx = base + j
            row = ts[p][gj][j % SUBLANES:j % SUBLANES + 1, :]
            for g in (range(0, gj + 1) if rev else range(gj, ng)):
                col = jnp.take_along_axis(n_scr[p, SUBLANES * g:SUBLANES * (g + 1), :], idx, axis=1)
                ts[p][g] = ts[p][g] - col * row
    return [jnp.concatenate(t, axis=0) for t in ts]


def _split3_dot(a, b):
    a_hi = a.astype(bf16)
    a_lo = (a - a_hi.astype(f32)).astype(bf16)
    b_hi = b.astype(bf16)
    b_lo = (b - b_hi.astype(f32)).astype(bf16)
    return jnp.dot(jnp.concatenate([a_hi, a_hi, a_lo], axis=1), jnp.concatenate([b_hi, b_lo, b_hi], axis=0),
                   preferred_element_type=f32)


def _block_diag2(a, b):
    z = jnp.zeros_like(a)
    return jnp.concatenate([jnp.concatenate([a, z], axis=1), jnp.concatenate([z, b], axis=1)], axis=0)


def _dn_intra_body(q_ref, k_ref, v_ref, gb_ref, gbt_ref, uc_ref, wq_ref, qkkd_ref, np_scr, n_scr):
    C = DN_CHUNK
    H = DN_HALF
    ri = lax.broadcasted_iota(jnp.int32, (C, 2 * C), 0)
    ci = lax.broadcasted_iota(jnp.int32, (C, 2 * C), 1)
    cl = ci % C
    incl = (ri >= cl, ri <= cl)
    strict = (ri > cl, ri < cl)
    left = ci < C
    cih = lax.broadcasted_iota(jnp.int32, (H, 2 * C), 1)
    lo_h = (cih % C) < H
    left_h = cih < C
    zero_h = jnp.zeros((H, 2 * C), f32)
    last = (C - 1, 0)
    gb = gb_ref[0]
    er = lax.broadcasted_iota(jnp.int32, (DN_DK, DN_DK), 0)
    ec = lax.broadcasted_iota(jnp.int32, (DN_DK, DN_DK), 1)
    eye_dk = (er == ec).astype(bf16)

    for pr in range(DN_PAIRS):
        h0, h1 = 2 * pr, 2 * pr + 1
        s0 = slice(h0 * DN_DK, (h0 + 1) * DN_DK)
        s1 = slice(h1 * DN_DK, (h1 + 1) * DN_DK)
        k0 = k_ref[0, :, s0].astype(bf16)
        k1 = k_ref[0, :, s1].astype(bf16)
        kbd = _block_diag2(k0, k1)
        kk_p = lax.dot_general(jnp.concatenate([k0, k1], axis=1), kbd, (((1,), (1,)), ((), ())),
                               preferred_element_type=f32)
        qcat = jnp.concatenate([q_ref[0, :, s0], q_ref[0, :, s1]], axis=1).astype(bf16)
        qk_p = lax.dot_general(qcat, kbd, (((1,), (1,)), ((), ())), preferred_element_type=f32)
        kt_p = lax.dot_general(eye_dk, jnp.concatenate([k0, k1], axis=0), (((1,), (1,)), ((), ())),
                               preferred_element_type=f32)
        for d in range(2):
            bl0, bl1 = d * DN_HEADS + h0, d * DN_HEADS + h1
            gl0, gl1 = 2 * DN_HEADS + bl0, 2 * DN_HEADS + bl1
            beta_p = jnp.take_along_axis(gb, jnp.where(left, bl0, bl1), axis=1)
            gc_p = jnp.take_along_axis(gb, jnp.where(left, gl0, gl1), axis=1)
            gc_row = jnp.concatenate([gbt_ref[0, 0, gl0:gl0 + 1, :], gbt_ref[0, 0, gl1:gl1 + 1, :]], axis=1)
            decay = jnp.exp(jnp.where(incl[d], gc_p - gc_row, -jnp.inf))
            n_p = jnp.where(strict[d], (beta_p * kk_p) * decay, 0.0)
            p = pr * 2 + d
            np_scr[p] = n_p
            n_scr[p] = jnp.where(lo_h, n_p[:H], n_p[H:])
            qkkd_ref[d, 0, 0, pr, 0:C, :] = jnp.where(incl[d], qk_p * decay, 0.0).astype(bf16)
            g_end = gc_p[last[d]:last[d] + 1, :]
            qkkd_ref[d, 0, 0, pr, C:C + DN_DK, :] = (kt_p * jnp.exp(g_end - gc_row)).astype(bf16)

    t_packs = _diag_block_inverses(n_scr, [False, True] * DN_PAIRS)

    def scaled(hh, d):
        hs = slice(hh * DN_DK, (hh + 1) * DN_DK)
        bl = d * DN_HEADS + hh
        full = jnp.zeros((C, LANES), jnp.int32)
        beta = jnp.take_along_axis(gb, full + bl, axis=1)
        egc = jnp.exp(jnp.take_along_axis(gb, full + (2 * DN_HEADS + bl), axis=1))
        wq_ref[d, 0, 0, C:2 * C, hs] = (q_ref[0, :, hs] * egc).astype(bf16)
        return jnp.concatenate([v_ref[0, :, hs] * beta, k_ref[0, :, hs] * (beta * egc)], axis=1)

    for pr in range(DN_PAIRS):
        h0, h1 = 2 * pr, 2 * pr + 1
        s0 = slice(h0 * DN_DK, (h0 + 1) * DN_DK)
        s1 = slice(h1 * DN_DK, (h1 + 1) * DN_DK)
        for d in range(2):
            p = pr * 2 + d
            tp = t_packs[p]
            if d == 0:
                nb = np_scr[p, H:C, :]
                r1 = jnp.concatenate([zero_h, jnp.where(jnp.logical_and(lo_h, left_h), nb, 0.0), zero_h,
                                      jnp.where(jnp.logical_and(lo_h, jnp.logical_not(left_h)), nb, 0.0)], axis=0)
                r2 = jnp.concatenate([jnp.where(cih < H, tp, 0.0), zero_h,
                                      jnp.where(jnp.logical_and(cih >= C, cih < C + H), tp, 0.0), zero_h], axis=0)
                y = _split3_dot(_split3_dot(tp, r1), r2)
                t_top = jnp.where(lo_h, tp, 0.0)
                t_bot = jnp.where(lo_h, -y, tp)
            else:
                nt = np_scr[p, 0:H, :]
                hi_h = jnp.logical_not(lo_h)
                r1 = jnp.concatenate([jnp.where(jnp.logical_and(hi_h, left_h), nt, 0.0), zero_h,
                                      jnp.where(jnp.logical_and(hi_h, jnp.logical_not(left_h)), nt, 0.0), zero_h],
                                     axis=0)
                r2 = jnp.concatenate([zero_h, jnp.where(jnp.logical_and(cih >= H, cih < C), tp, 0.0),
                                      zero_h, jnp.where(cih >= C + H, tp, 0.0)], axis=0)
                y = _split3_dot(_split3_dot(tp, r1), r2)
                t_top = jnp.where(lo_h, tp, -y)
                t_bot = jnp.where(lo_h, 0.0, tp)
            t_p = jnp.concatenate([t_top, t_bot], axis=0)
            sol = _split3_dot(t_p, _block_diag2(scaled(h0, d), scaled(h1, d)))
            uc_ref[d, 0, :, s0] = sol[:, 0:DN_DV]
            wq_ref[d, 0, 0, 0:C, s0] = sol[:, DN_DV:2 * DN_DV].astype(bf16)
            uc_ref[d, 0, :, s1] = sol[:, 2 * DN_DV:3 * DN_DV]
            wq_ref[d, 0, 0, 0:C, s1] = sol[:, 3 * DN_DV:4 * DN_DV].astype(bf16)


def _dn_intra(q, k, v, gb, gbt):
    B, Lt, W = q.shape
    nc = Lt // DN_CHUNK
    C = DN_CHUNK
    tok = lambda w: pl.BlockSpec((1, C, w), lambda b, i: (b, i, 0))
    return pl.pallas_call(
        _dn_intra_body,
        out_shape=(jax.ShapeDtypeStruct((2, B, Lt, W), f32),
                   jax.ShapeDtypeStruct((2, B, nc, 2 * C, W), bf16),
                   jax.ShapeDtypeStruct((2, B, nc, DN_PAIRS, C + DN_DK, 2 * C), bf16)),
        grid=(B, nc),
        in_specs=[tok(W), tok(W), tok(W), tok(LANES),
                  pl.BlockSpec((1, 1, 16, C), lambda b, i: (b, i, 0, 0))],
        out_specs=(pl.BlockSpec((2, 1, C, W), lambda b, i: (0, b, i, 0)),
                   pl.BlockSpec((2, 1, 1, 2 * C, W), lambda b, i: (0, b, i, 0, 0)),
                   pl.BlockSpec((2, 1, 1, DN_PAIRS, C + DN_DK, 2 * C), lambda b, i: (0, b, i, 0, 0, 0))),
        scratch_shapes=[pltpu.VMEM((2 * DN_PAIRS, C, 2 * C), f32),
                        pltpu.VMEM((2 * DN_PAIRS, DN_HALF, 2 * C), f32)],
        compiler_params=_cparams(("arbitrary", "arbitrary")),
        name="dn_intra",
    )(q, k, v, gb, gbt)


def _dn_state_body(*refs, rev):
    if rev:
        uc_ref, wq_ref, qkkd_ref, gb_ref, of_ref, o_ref, s_scr = refs
    else:
        uc_ref, wq_ref, qkkd_ref, gb_ref, o_ref, s_scr = refs
    C = DN_CHUNK
    d = 1 if rev else 0
    last = 0 if rev else C - 1

    @pl.when(pl.program_id(1) == 0)
    def _():
        s_scr[...] = jnp.zeros_like(s_scr)

    for bb in range(DN_STATE_BATCH):
        for pr in range(DN_PAIRS):
            heads = (2 * pr, 2 * pr + 1)
            s_old, o_part, vbs = [], [], []
            for hh in heads:
                hs = slice(hh * DN_DK, (hh + 1) * DN_DK)
                s = s_scr[bb, hh]
                r = jnp.dot(wq_ref[0, bb, 0, :, hs], s.astype(bf16), preferred_element_type=f32)
                vbs.append((uc_ref[0, bb, :, hs] - r[0:C]).astype(bf16))
                o_part.append(r[C:2 * C])
                s_old.append(s)
            r2 = jnp.dot(qkkd_ref[0, bb, 0, pr], _block_diag2(vbs[0], vbs[1]), preferred_element_type=f32)
            for n, hh in enumerate(heads):
                hs = slice(hh * DN_DK, (hh + 1) * DN_DK)
                vs = slice(n * DN_DV, (n + 1) * DN_DV)
                gl = 2 * DN_HEADS + d * DN_HEADS + hh
                eg = jnp.exp(gb_ref[bb, last:last + 1, gl:gl + 1])
                s_scr[bb, hh] = s_old[n] * eg + r2[C:C + DN_DK, vs]
                o_h = o_part[n] + r2[0:C, vs]
                if rev:
                    o_h = o_h + of_ref[bb, :, hs]
                o_ref[bb, :, hs] = o_h


def _dn_state(uc, wq, qkkd, gb, n_ctx_chunks, rev, o_fwd=None):
    _, B, Lt, W = uc.shape
    C = DN_CHUNK
    nc = Lt // C
    nb = DN_STATE_BATCH
    d = 1 if rev else 0
    if rev:
        cidx = lambda i: jnp.where(i < n_ctx_chunks, n_ctx_chunks - 1 - i, nc - 1 - (i - n_ctx_chunks))
    else:
        cidx = lambda i: i
    tok = lambda w: pl.BlockSpec((nb, C, w), lambda b, i: (b, cidx(i), 0))
    in_specs = [pl.BlockSpec((1, nb, C, W), lambda b, i: (d, b, cidx(i), 0)),
                pl.BlockSpec((1, nb, 1, 2 * C, W), lambda b, i: (d, b, cidx(i), 0, 0)),
                pl.BlockSpec((1, nb, 1, DN_PAIRS, C + DN_DK, 2 * C), lambda b, i: (d, b, cidx(i), 0, 0, 0)),
                tok(LANES)]
    args = [uc, wq, qkkd, gb]
    if rev:
        in_specs.append(tok(W))
        args.append(o_fwd)
    return pl.pallas_call(
        functools.partial(_dn_state_body, rev=rev),
        out_shape=jax.ShapeDtypeStruct((B, Lt, W), f32),
        grid=(B // nb, nc),
        in_specs=in_specs,
        out_specs=tok(W),
        scratch_shapes=[pltpu.VMEM((nb, DN_HEADS, DN_DK, DN_DV), f32)],
        compiler_params=_cparams(("arbitrary", "arbitrary")),
        name="dn_state_bwd" if rev else "dn_state_fwd",
    )(*args)


def _attn_body(q_ref, kp_ref, kc_ref, kn_ref, kx_ref, vp_ref, vc_ref, vn_ref, vx_ref, sink_ref, o_ref,
               *, n_ctx_blocks, seq_len):
    Tq = AT_BLOCK
    G = AT_HEADS // AT_KV
    i = pl.program_id(1)
    is_ctx = i < n_ctx_blocks
    nloc = 3 * Tq
    qpos = (i - n_ctx_blocks) * Tq + lax.broadcasted_iota(jnp.int32, (Tq, nloc), 0)
    kpos = (i - n_ctx_blocks - 1) * Tq + lax.broadcasted_iota(jnp.int32, (Tq, nloc), 1)
    ok = jnp.logical_and(jnp.abs(qpos - kpos) <= WINDOW, jnp.logical_and(kpos >= 0, kpos < seq_len))
    ok = jnp.logical_and(ok, jnp.logical_not(is_ctx))
    lctx = kx_ref.shape[1]
    valid = jnp.concatenate([ok, jnp.ones((Tq, lctx), dtype=jnp.bool_)], axis=1)
    valid = jnp.concatenate([valid] * G, axis=0)
    scale = AT_HD ** -0.5
    for hk in range(AT_KV):
        ks = slice(hk * AT_HD, (hk + 1) * AT_HD)
        kall = jnp.concatenate([kp_ref[0, :, ks], kc_ref[0, :, ks], kn_ref[0, :, ks], kx_ref[0, :, ks]], axis=0)
        vall = jnp.concatenate([vp_ref[0, :, ks], vc_ref[0, :, ks], vn_ref[0, :, ks], vx_ref[0, :, ks]], axis=0)
        qs = jnp.concatenate([q_ref[0, :, (hk * G + g) * AT_HD:(hk * G + g + 1) * AT_HD] for g in range(G)],
                             axis=0)
        sink = jnp.concatenate([jnp.broadcast_to(sink_ref[0:1, hk * G + g:hk * G + g + 1], (Tq, 1))
                                for g in range(G)], axis=0)
        s = jnp.where(valid, _dot_nt(qs, kall) * scale, -jnp.inf)
        m = jnp.maximum(jnp.max(s, axis=-1, keepdims=True), sink)
        p = jnp.exp(s - m)
        denom = jnp.sum(p, axis=-1, keepdims=True) + jnp.exp(sink - m)
        o = _dot(p, vall) / denom
        for g in range(G):
            hq = hk * G + g
            o_ref[0, :, hq * AT_HD:(hq + 1) * AT_HD] = o[g * Tq:(g + 1) * Tq, :]


def _attention(aq, ak, av, sink_vec, n_ctx, seq_len):
    B, Lt, _ = aq.shape
    nb = Lt // AT_BLOCK
    n_ctx_blocks = n_ctx // AT_BLOCK
    blk = lambda w, f: pl.BlockSpec((1, AT_BLOCK, w), lambda b, i: (b, f(i), 0))
    lo, hi = n_ctx_blocks, nb - 1
    prev = lambda i: jnp.clip(i - 1, lo, hi)
    cur = lambda i: i
    nxt = lambda i: jnp.clip(i + 1, lo, hi)
    ctx_spec = pl.BlockSpec((1, n_ctx, AT_KW), lambda b, i: (b, 0, 0))
    body = functools.partial(_attn_body, n_ctx_blocks=n_ctx_blocks, seq_len=seq_len)
    return pl.pallas_call(
        body,
        out_shape=jax.ShapeDtypeStruct((B, Lt, AT_QW), f32),
        grid=(B, nb),
        in_specs=[blk(AT_QW, cur),
                  blk(AT_KW, prev), blk(AT_KW, cur), blk(AT_KW, nxt), ctx_spec,
                  blk(AT_KW, prev), blk(AT_KW, cur), blk(AT_KW, nxt), ctx_spec,
                  _const_spec((1, LANES))],
        out_specs=blk(AT_QW, cur),
        compiler_params=_cparams(("arbitrary", "arbitrary")),
        name="window_attn",
    )(aq, ak, ak, ak, ak, av, av, av, av, sink_vec)


def _merge_body(x_ref, mod_ref, g_ref, ys_ref, od_ref, z_ref, yc_ref,
                wg_ref, wglu_ref, ng_ref, wa_ref, wb_ref, wc_ref, wo_ref, o_ref):
    x = x_ref[0]
    h = _modnorm(x, g_ref[...], mod_ref[0, 1:2, :], mod_ref[0, 0:1, :]).astype(bf16)
    zs = jax.nn.gelu(ys_ref[...])
    ya = zs * jax.nn.sigmoid(_dot(zs, wglu_ref[...]))
    ng = ng_ref[...]
    parts = []
    for hh in range(DN_HEADS):
        hs = slice(hh * DN_DV, (hh + 1) * DN_DV)
        o = od_ref[0, :, hs]
        on = o * lax.rsqrt(jnp.mean(o * o, axis=-1, keepdims=True) + EPS) * ng
        parts.append(on * _silu(z_ref[0, :, hs]))
    yb = jnp.concatenate(parts, axis=1)
    yc = yc_ref[0]
    D = D_MODEL

    def gate(j):
        return jax.nn.sigmoid(jnp.dot(h, wg_ref[:, j * D:(j + 1) * D], preferred_element_type=f32))

    m = gate(0) * _dot(ya, wa_ref[...])
    m = m + gate(1) * _dot(yb, wb_ref[...])
    m = m + gate(2) * _dot(yc, wc_ref[...])
    mix = _dot(m, wo_ref[...])
    o_ref[0] = x + mod_ref[0, 2:3, :] * mix


def _merge(xcat, mods, g1, ys_tm, o_dn, dz, yc, w_gate, w_glu, dn_g, w_a, w_b, w_c, w_o, n_ctx_tiles):
    B, Lt, D = xcat.shape
    nt = Lt // TOK_TILE
    tok = lambda w: pl.BlockSpec((1, TOK_TILE, w), lambda b, t: (b, t, 0))
    return pl.pallas_call(
        _merge_body,
        out_shape=jax.ShapeDtypeStruct((B, Lt, D), f32),
        grid=(B, nt),
        in_specs=[tok(D),
                  pl.BlockSpec((1, N_MOD, D), _mod_index(n_ctx_tiles, B)),
                  _const_spec((1, D)),
                  pl.BlockSpec((TOK_TILE, SSM_WIDTH), lambda b, t: (t, b)),
                  tok(DN_W), tok(DN_W), tok(AT_QW),
                  _const_spec(w_gate.shape), _const_spec(w_glu.shape), _const_spec((1, DN_DV)),
                  _const_spec(w_a.shape), _const_spec(w_b.shape), _const_spec(w_c.shape),
                  _const_spec(w_o.shape)],
        out_specs=tok(D),
        compiler_params=_cparams(("arbitrary", "arbitrary")),
        name="merge_out",
    )(xcat, mods, g1, ys_tm, o_dn, dz, yc, w_gate, w_glu, dn_g, w_a, w_b, w_c, w_o)


FF_CHUNK = 1024


def _ffn_body(x_ref, mod_ref, g_ref, w1_ref, w2_ref, o_ref):
    x = x_ref[0]
    h = _modnorm(x, g_ref[...], mod_ref[0, 4:5, :], mod_ref[0, 3:4, :]).astype(bf16)
    acc = jnp.zeros(x.shape, f32)
    for j in range(D_FF // FF_CHUNK):
        a = jnp.dot(h, w1_ref[:, j * FF_CHUNK:(j + 1) * FF_CHUNK], preferred_element_type=f32)
        a = jnp.square(jnp.maximum(a, 0.0)).astype(bf16)
        acc = acc + jnp.dot(a, w2_ref[j * FF_CHUNK:(j + 1) * FF_CHUNK, :], preferred_element_type=f32)
    o_ref[0] = x + mod_ref[0, 5:6, :] * acc


def _ffn(xcat, mods, g2, w1, w2, n_ctx_tiles):
    B, Lt, D = xcat.shape
    nt = Lt // TOK_TILE
    tok = pl.BlockSpec((1, TOK_TILE, D), lambda b, t: (b, t, 0))
    return pl.pallas_call(
        _ffn_body,
        out_shape=jax.ShapeDtypeStruct((B, Lt, D), f32),
        grid=(B, nt),
        in_specs=[tok, pl.BlockSpec((1, N_MOD, D), _mod_index(n_ctx_tiles, B)), _const_spec((1, D)),
                  _const_spec(w1.shape), _const_spec(w2.shape)],
        out_specs=tok,
        compiler_params=_cparams(("arbitrary", "arbitrary")),
        name="ffn",
    )(xcat, mods, g2, w1, w2)


def _final_body(x_ref, g_ref, o_ref):
    x = x_ref[0]
    o_ref[0] = x * lax.rsqrt(jnp.mean(x * x, axis=-1, keepdims=True) + EPS) * g_ref[...]


def _final_norm(xcat, g, n_ctx_tiles, seq_len):
    B, Lt, D = xcat.shape
    return pl.pallas_call(
        _final_body,
        out_shape=jax.ShapeDtypeStruct((B, seq_len, D), f32),
        grid=(B, seq_len // TOK_TILE),
        in_specs=[pl.BlockSpec((1, TOK_TILE, D), lambda b, t: (b, t + n_ctx_tiles, 0)), _const_spec((1, D))],
        out_specs=pl.BlockSpec((1, TOK_TILE, D), lambda b, t: (b, t, 0)),
        compiler_params=_cparams(("arbitrary", "arbitrary")),
        name="final_norm",
    )(xcat, g)


def _rope_tables(n_ctx, seq_len):
    n = AT_HD // 4
    inv_freq = ROPE_BASE ** (-jnp.arange(n, dtype=f32) / n)
    pos = jnp.arange(seq_len, dtype=jnp.int32)
    rows = (pos // GRID_W).astype(f32)[:, None] * inv_freq[None, :]
    cols = (pos % GRID_W).astype(f32)[:, None] * inv_freq[None, :]
    cos = jnp.concatenate([jnp.cos(rows)] * 2 + [jnp.cos(cols)] * 2, axis=1)
    sin = jnp.concatenate([-jnp.sin(rows), jnp.sin(rows), -jnp.sin(cols), jnp.sin(cols)], axis=1)
    cos = jnp.concatenate([jnp.ones((n_ctx, AT_HD), f32), cos], axis=0)
    sin = jnp.concatenate([jnp.zeros((n_ctx, AT_HD), f32), sin], axis=0)
    return jnp.tile(cos, (1, LANES // AT_HD)), jnp.tile(sin, (1, LANES // AT_HD))


def _split_w_in(w_in):
    o = SSM_WIDTH + 4 * DN_W
    w_ba = w_in[:, o:o + 4 * DN_HEADS]
    o2 = o + 4 * DN_HEADS
    w_att = w_in[:, o2:o2 + AT_QW + 2 * AT_KW]
    w_gate = w_in[:, o2 + AT_QW + 2 * AT_KW:]
    w_main = jnp.concatenate([w_in[:, :o], w_att], axis=1).astype(bf16)
    w_ba = jnp.pad(w_ba, ((0, 0), (0, LANES - 4 * DN_HEADS))).astype(bf16)
    return w_main, w_ba, w_gate.astype(bf16)


def kernel(x, c, ctx, c_ctx, norm1_g, norm2_g, w_mod, b_mod, w_in, ssm_lam_re, ssm_lam_im, ssm_log_dt, ssm_b_re, ssm_b_im, ssm_c_re, ssm_c_im, ssm_d, ssm_w_glu, dn_conv_w, dn_a_log, dn_dt_bias, dn_norm_g, attn_sink, w_branch_a, w_branch_b, w_branch_c, w_out, w_ff1, w_ff2, final_norm_g):
    B, L, D = x.shape
    Lc = ctx.shape[1]
    depth = w_in.shape[0]
    assert B == SUBLANES and D == D_MODEL
    assert Lc % TOK_TILE == 0 and L % TOK_TILE == 0
    Lt = Lc + L
    n_ctx_tiles = Lc // TOK_TILE

    cond = jnp.concatenate([c, c_ctx[None, :], jnp.zeros((16 - B - 1, D), f32)], axis=0)
    mods_all = _modulation(cond, w_mod, b_mod).reshape(depth, 16, N_MOD, D)
    cos_t, sin_t = _rope_tables(Lc, L)
    xcat = jnp.concatenate([ctx, x], axis=1)

    for layer in range(depth):
        mods = mods_all[layer]
        g1 = norm1_g[layer].reshape(1, D)
        g2 = norm2_g[layer].reshape(1, D)
        w_main, w_ba, w_gate = _split_w_in(w_in[layer])

        u_tm, dqkv, dz, dba, aq, ak, av = _inproj(xcat, mods, g1, w_main, w_ba, cos_t, sin_t, n_ctx_tiles)

        a_re, a_im, bbt_re, bbt_im = _s5_discretize(ssm_lam_re[layer], ssm_lam_im[layer], ssm_log_dt[layer],
                                                    ssm_b_re[layer], ssm_b_im[layer])
        bmat, cmat, avec = _s5_matrices(a_re, a_im, bbt_re, bbt_im, ssm_c_re[layer], ssm_c_im[layer])
        u_rows = u_tm.reshape(Lt * B, SSM_WIDTH)
        y_f = _s5_scan(u_rows, bmat, cmat, avec, Lc, rev=False)
        y_s5 = _s5_scan(u_rows, bmat, cmat, avec, Lc, rev=True, yf=y_f,
                        d_skip=ssm_d[layer].reshape(1, SSM_WIDTH))
        ys_tm = y_s5.reshape(Lt, B * SSM_WIDTH)

        conv_w = jnp.pad(dn_conv_w[layer], ((0, SUBLANES - DN_CONV), (0, 0)))
        pad_l = 2 * DN_HEADS
        al_vec = jnp.pad(dn_a_log[layer].reshape(1, 2 * DN_HEADS), ((0, 0), (pad_l, LANES - 2 * pad_l)))
        dtb_vec = jnp.pad(dn_dt_bias[layer].reshape(1, 2 * DN_HEADS), ((0, 0), (pad_l, LANES - 2 * pad_l)))
        qn, kn, vs, gb, gbt = _dn_prepare(dqkv, dba, conv_w, al_vec, dtb_vec, n_ctx_tiles)
        n_ctx_chunks = Lc // DN_CHUNK
        uc, wq, qkkd = _dn_intra(qn, kn, vs, gb, gbt)
        o_f = _dn_state(uc, wq, qkkd, gb, n_ctx_chunks, rev=False)
        o_dn = _dn_state(uc, wq, qkkd, gb, n_ctx_chunks, rev=True, o_fwd=o_f)

        sink_vec = jnp.pad(attn_sink[layer].reshape(1, AT_HEADS), ((0, 0), (0, LANES - AT_HEADS)))
        yc = _attention(aq, ak, av, sink_vec, Lc, L)

        x1 = _merge(xcat, mods, g1, ys_tm, o_dn, dz, yc, w_gate, ssm_w_glu[layer].astype(bf16),
                    dn_norm_g[layer].reshape(1, DN_DV), w_branch_a[layer].astype(bf16),
                    w_branch_b[layer].astype(bf16), w_branch_c[layer].astype(bf16),
                    w_out[layer].astype(bf16), n_ctx_tiles)
        xcat = _ffn(x1, mods, g2, w_ff1[layer].astype(bf16), w_ff2[layer].astype(bf16), n_ctx_tiles)

    return _final_norm(xcat, final_norm_g.reshape(1, D), n_ctx_tiles, L)
```

```python
import functools
import math

import jax
import jax.numpy as jnp
from jax import lax
from jax.experimental import pallas as pl
from jax.experimental.pallas import tpu as pltpu

f32 = jnp.float32
bf16 = jnp.bfloat16
HI = lax.Precision.HIGHEST

D_MODEL = 1024
GRID_W = 64
EPS = 1e-6
SSM_WIDTH = D_MODEL // 2
SSM_GROUP = 16
SSM_GROUPS = SSM_WIDTH // SSM_GROUP
SSM_STATE = 64
DN_HEADS = 4
DN_DK = 128
DN_DV = 128
DN_CONV = 5
DN_CHUNK = 64
AT_HEADS = 8
AT_KV = 2
AT_HD = 64
WINDOW = 128
AT_BLOCK = 128
ROPE_BASE = 10000.0
D_FF = 4 * D_MODEL
N_MOD = 6
DN_W = DN_HEADS * DN_DK
AT_QW = AT_HEADS * AT_HD
AT_KW = AT_KV * AT_HD

SUBLANES = 8
LANES = 128
TOK_TILE = 256
S5_STEPS = 32
S5_COLBLK = 128
S5_NBLK = SSM_WIDTH // S5_COLBLK
S5_SBLK = (S5_COLBLK // SSM_GROUP) * SSM_STATE
VMEM_LIMIT = 56 * 1024 * 1024


def _cparams(sem):
    return pltpu.CompilerParams(dimension_semantics=sem, vmem_limit_bytes=VMEM_LIMIT)


def _const_spec(shape):
    nd = len(shape)
    return pl.BlockSpec(shape, lambda *_: (0,) * nd, pipeline_mode=pl.Buffered(1))


def _dot(a, b):
    return jnp.dot(a.astype(bf16), b.astype(bf16), preferred_element_type=f32)


def _dot_nt(a, b):
    return lax.dot_general(a.astype(bf16), b.astype(bf16), (((1,), (1,)), ((), ())),
                           preferred_element_type=f32)


def _dot_tn(a, b):
    return lax.dot_general(a.astype(bf16), b.astype(bf16), (((0,), (0,)), ((), ())),
                           preferred_element_type=f32)


def _modnorm(x, g, scale, shift):
    y = x * lax.rsqrt(jnp.mean(x * x, axis=-1, keepdims=True) + EPS)
    return (y * g) * (1.0 + scale) + shift


def _silu(x):
    return x * jax.nn.sigmoid(x)


def _mod_body(c_ref, w_ref, b_ref, o_ref):
    s = _silu(c_ref[...])
    o_ref[0] = jnp.dot(s, w_ref[0], precision=HI, preferred_element_type=f32) + b_ref[0]


def _modulation(cond, w_mod, b_mod):
    depth = w_mod.shape[0]
    nblk = (N_MOD * D_MODEL) // D_MODEL
    return pl.pallas_call(
        _mod_body,
        out_shape=jax.ShapeDtypeStruct((depth, 16, N_MOD * D_MODEL), f32),
        grid=(depth, nblk),
        in_specs=[pl.BlockSpec((16, D_MODEL), lambda l, j: (0, 0)),
                  pl.BlockSpec((1, D_MODEL, D_MODEL), lambda l, j: (l, 0, j)),
                  pl.BlockSpec((1, 1, D_MODEL), lambda l, j: (l, 0, j))],
        out_specs=pl.BlockSpec((1, 16, D_MODEL), lambda l, j: (l, 0, j)),
        compiler_params=_cparams(("arbitrary", "arbitrary")),
        name="adaln_mod",
    )(cond, w_mod, b_mod.reshape(depth, 1, N_MOD * D_MODEL))


def _rope(x, cos, sin):
    n = x.shape[-1]
    lane = lax.broadcasted_iota(jnp.int32, x.shape, 1)
    partner = jnp.where((lane % 32) < 16, pltpu.roll(x, n - 16, 1), pltpu.roll(x, 16, 1))
    return x * cos + partner * sin


def _inproj_body(x_ref, mod_ref, g_ref, wm_ref, wba_ref, cos_ref, sin_ref,
                 u_ref, qkv_ref, z_ref, ba_ref, aq_ref, ak_ref, av_ref):
    h = _modnorm(x_ref[0], g_ref[...], mod_ref[0, 1:2, :], mod_ref[0, 0:1, :]).astype(bf16)

    def proj(lo, hi):
        return jnp.dot(h, wm_ref[:, lo:hi], preferred_element_type=f32)

    o = 0
    u_ref[...] = proj(o, o + SSM_WIDTH)
    o += SSM_WIDTH
    qkv_ref[0] = proj(o, o + 3 * DN_W)
    o += 3 * DN_W
    z_ref[0] = proj(o, o + DN_W)
    o += DN_W
    cos = cos_ref[...]
    sin = sin_ref[...]
    reps = AT_QW // LANES
    aq_ref[0] = _rope(proj(o, o + AT_QW), jnp.concatenate([cos] * reps, axis=1),
                      jnp.concatenate([sin] * reps, axis=1)) * (AT_HD ** -0.5)
    o += AT_QW
    ak_ref[0] = _rope(proj(o, o + AT_KW), cos, sin)
    o += AT_KW
    av_ref[0] = proj(o, o + AT_KW)
    ba_ref[0] = jnp.dot(h, wba_ref[...], preferred_element_type=f32)


def _mod_index(n_ctx_tiles, ctx_row):
    return lambda b, t: (jnp.where(t < n_ctx_tiles, ctx_row, b), 0, 0)


def _inproj(xcat, mods, g1, w_main, w_ba, cos_t, sin_t, n_ctx_tiles):
    B, Lt, D = xcat.shape
    nt = Lt // TOK_TILE
    wm = w_main.shape[1]
    tok = lambda w: pl.BlockSpec((1, TOK_TILE, w), lambda b, t: (b, t, 0))
    out_shape = (
        jax.ShapeDtypeStruct((Lt, B * SSM_WIDTH), f32),
        jax.ShapeDtypeStruct((B, Lt, 3 * DN_W), f32),
        jax.ShapeDtypeStruct((B, Lt, DN_W), f32),
        jax.ShapeDtypeStruct((B, Lt, LANES), f32),
        jax.ShapeDtypeStruct((B, Lt, AT_QW), f32),
        jax.ShapeDtypeStruct((B, Lt, AT_KW), f32),
        jax.ShapeDtypeStruct((B, Lt, AT_KW), f32),
    )
    return pl.pallas_call(
        _inproj_body,
        out_shape=out_shape,
        grid=(B, nt),
        in_specs=[tok(D),
                  pl.BlockSpec((1, N_MOD, D), _mod_index(n_ctx_tiles, B)),
                  _const_spec((1, D)),
                  _const_spec((D, wm)),
                  _const_spec((D, LANES)),
                  pl.BlockSpec((TOK_TILE, LANES), lambda b, t: (t, 0)),
                  pl.BlockSpec((TOK_TILE, LANES), lambda b, t: (t, 0))],
        out_specs=(pl.BlockSpec((TOK_TILE, SSM_WIDTH), lambda b, t: (t, b)),
                   tok(3 * DN_W), tok(DN_W), tok(LANES), tok(AT_QW), tok(AT_KW), tok(AT_KW)),
        compiler_params=_cparams(("arbitrary", "arbitrary")),
        name="in_proj",
    )(xcat, mods, g1, w_main, w_ba, cos_t, sin_t)


def _s5disc_body(lre_ref, lim_ref, ldt_ref, bre_ref, bim_ref, are_ref, aim_ref, bbre_ref, bbim_ref):
    lr = lre_ref[0]
    li = lim_ref[0]
    dt = jnp.exp(ldt_ref[0])
    mag = jnp.exp(lr * dt)
    a_re = mag * jnp.cos(li * dt)
    a_im = mag * jnp.sin(li * dt)
    den = lr * lr + li * li
    f_re = ((a_re - 1.0) * lr + a_im * li) / den
    f_im = (a_im * lr - (a_re - 1.0) * li) / den
    are_ref[0] = a_re
    aim_ref[0] = a_im
    b_re = bre_ref[0]
    b_im = bim_ref[0]
    bbre_ref[0] = f_re * b_re - f_im * b_im
    bbim_ref[0] = f_re * b_im + f_im * b_re


def _s5_discretize(lam_re, lam_im, log_dt, b_re, b_im):
    n = 2 * SSM_GROUPS
    P, H = SSM_STATE, SSM_GROUP
    vec = pl.BlockSpec((1, 1, P), lambda i: (i, 0, 0))
    mat = pl.BlockSpec((1, H, P), lambda i: (i, 0, 0))
    return pl.pallas_call(
        _s5disc_body,
        out_shape=(jax.ShapeDtypeStruct((n, 1, P), f32), jax.ShapeDtypeStruct((n, 1, P), f32),
                   jax.ShapeDtypeStruct((n, H, P), f32), jax.ShapeDtypeStruct((n, H, P), f32)),
        grid=(n,),
        in_specs=[vec, vec, pl.BlockSpec((1, 1, 1), lambda i: (i, 0, 0)), mat, mat],
        out_specs=(vec, vec, mat, mat),
        compiler_params=_cparams(("arbitrary",)),
        name="s5_discretize",
    )(lam_re.reshape(n, 1, P), lam_im.reshape(n, 1, P), log_dt.reshape(n, 1, 1),
      jnp.swapaxes(b_re, -1, -2).reshape(n, H, P), jnp.swapaxes(b_im, -1, -2).reshape(n, H, P))


def _s5_matrices(a_re, a_im, bbt_re, bbt_im, c_re, c_im):
    gb = S5_COLBLK // SSM_GROUP
    P, H = SSM_STATE, SSM_GROUP
    eye = jnp.eye(gb, dtype=f32)

    def bblk(t):
        t = t.reshape(2, S5_NBLK, gb, H, P)
        return jnp.einsum('dcghp,gk->dcghkp', t, eye).reshape(2, S5_NBLK, gb * H, gb * P)

    def cblk(t):
        t = t.reshape(2, S5_NBLK, gb, H, P)
        return jnp.einsum('dcghp,gk->dcgpkh', t, eye).reshape(2, S5_NBLK, gb * P, gb * H)

    bmat = jnp.concatenate([bblk(bbt_re), bblk(bbt_im)], axis=-1).astype(bf16)
    cmat = jnp.concatenate([cblk(c_re.astype(f32)), -cblk(c_im.astype(f32))], axis=-2).astype(bf16)
    avec = jnp.stack([a_re.reshape(2, S5_NBLK, gb * P), a_im.reshape(2, S5_NBLK, gb * P)], axis=2)
    return bmat, cmat, avec


def _s5_scan_body(*refs, rev):
    if rev:
        u_ref, bm_ref, cm_ref, a_ref, yf_ref, d_ref, y_ref, bu_scr, st_scr = refs
    else:
        u_ref, bm_ref, cm_ref, a_ref, y_ref, bu_scr, st_scr = refs
    B = SUBLANES
    nsb = S5_SBLK

    @pl.when(pl.program_id(0) == 0)
    def _():
        st_scr[...] = jnp.zeros_like(st_scr)

    for cb in range(S5_NBLK):
        cols = slice(cb * S5_COLBLK, (cb + 1) * S5_COLBLK)
        bu_scr[...] = jnp.dot(u_ref[:, cols].astype(bf16), bm_ref[0, cb], preferred_element_type=f32)
        are = jnp.broadcast_to(a_ref[0, cb, 0:1, :], (B, nsb))
        aim = jnp.broadcast_to(a_ref[0, cb, 1:2, :], (B, nsb))

        def step(i, carry, are=are, aim=aim):
            sre, sim = carry
            t = (S5_STEPS - 1 - i) if rev else i
            r0 = pl.multiple_of(t * B, B)
            nre = are * sre - aim * sim + bu_scr[pl.ds(r0, B), 0:nsb]
            nim = are * sim + aim * sre + bu_scr[pl.ds(r0, B), nsb:2 * nsb]
            bu_scr[pl.ds(r0, B), 0:nsb] = nre
            bu_scr[pl.ds(r0, B), nsb:2 * nsb] = nim
            return nre, nim

        sre, sim = lax.fori_loop(0, S5_STEPS, step, (st_scr[cb, :, 0:nsb], st_scr[cb, :, nsb:2 * nsb]),
                                 unroll=True)
        st_scr[cb, :, 0:nsb] = sre
        st_scr[cb, :, nsb:2 * nsb] = sim
        y = jnp.dot(bu_scr[...].astype(bf16), cm_ref[0, cb], preferred_element_type=f32)
        if rev:
            y = y + yf_ref[:, cols] + d_ref[:, cols] * u_ref[:, cols]
        y_ref[:, cols] = y


def _s5_tile_index(n_ctx_tiles, n_tiles, rev):
    if not rev:
        return lambda i: (i, 0)
    return lambda i: (jnp.where(i < n_ctx_tiles, n_ctx_tiles - 1 - i, n_tiles - 1 - (i - n_ctx_tiles)), 0)


def _s5_scan(u_rows, bmat, cmat, avec, n_ctx_steps, rev, yf=None, d_skip=None):
    n_rows = u_rows.shape[0]
    rows = S5_STEPS * SUBLANES
    n_tiles = n_rows // rows
    n_ctx_tiles = n_ctx_steps // S5_STEPS
    idx = _s5_tile_index(n_ctx_tiles, n_tiles, rev)
    d = 1 if rev else 0
    tile = pl.BlockSpec((rows, SSM_WIDTH), idx)
    in_specs = [tile,
                pl.BlockSpec((1, S5_NBLK, S5_COLBLK, 2 * S5_SBLK), lambda i: (d, 0, 0, 0), pipeline_mode=pl.Buffered(1)),
                pl.BlockSpec((1, S5_NBLK, 2 * S5_SBLK, S5_COLBLK), lambda i: (d, 0, 0, 0), pipeline_mode=pl.Buffered(1)),
                pl.BlockSpec((1, S5_NBLK, 2, S5_SBLK), lambda i: (d, 0, 0, 0), pipeline_mode=pl.Buffered(1))]
    args = [u_rows, bmat, cmat, avec]
    if rev:
        in_specs += [tile, _const_spec((1, SSM_WIDTH))]
        args += [yf, d_skip]
    return pl.pallas_call(
        functools.partial(_s5_scan_body, rev=rev),
        out_shape=jax.ShapeDtypeStruct((n_rows, SSM_WIDTH), f32),
        grid=(n_tiles,),
        in_specs=in_specs,
        out_specs=tile,
        scratch_shapes=[pltpu.VMEM((rows, 2 * S5_SBLK), f32),
                        pltpu.VMEM((S5_NBLK, SUBLANES, 2 * S5_SBLK), f32)],
        compiler_params=_cparams(("arbitrary",)),
        name="s5_scan_bwd" if rev else "s5_scan_fwd",
    )(*args)


def _dn_prep_body(qkv_ref, prev_ref, next_ref, ba_ref, cw_ref, al_ref, dtb_ref,
                  q_ref, k_ref, v_ref, gb_ref, gbt_ref, ext_scr, *, n_ctx_tiles, n_tiles):
    T = TOK_TILE
    t = pl.program_id(1)
    prev_ok = jnp.logical_and(t != 0, t != n_ctx_tiles)
    next_ok = jnp.logical_and(t != n_ctx_tiles - 1, t != n_tiles - 1)
    ext_scr[0:SUBLANES, :] = jnp.where(prev_ok, prev_ref[0], 0.0)
    ext_scr[SUBLANES:SUBLANES + T, :] = qkv_ref[0]
    ext_scr[SUBLANES + T:2 * SUBLANES + T, :] = jnp.where(next_ok, next_ref[0], 0.0)
    half = DN_CONV // 2
    acc = cw_ref[0:1, :] * ext_scr[pl.ds(SUBLANES - half, T), :]
    for kk in range(1, DN_CONV):
        acc = acc + cw_ref[kk:kk + 1, :] * ext_scr[pl.ds(SUBLANES - half + kk, T), :]
    act = _silu(acc)
    for hh in range(DN_HEADS):
        qs = act[:, hh * DN_DK:(hh + 1) * DN_DK]
        ks = act[:, DN_W + hh * DN_DK:DN_W + (hh + 1) * DN_DK]
        q_ref[0, :, hh * DN_DK:(hh + 1) * DN_DK] = qs * (lax.rsqrt(jnp.sum(qs * qs, axis=-1, keepdims=True) + EPS)
                                                       * (DN_DK ** -0.5))
        k_ref[0, :, hh * DN_DK:(hh + 1) * DN_DK] = ks * lax.rsqrt(jnp.sum(ks * ks, axis=-1, keepdims=True) + EPS)
    v_ref[0] = act[:, 2 * DN_W:3 * DN_W]

    ba = ba_ref[0]
    lane = lax.broadcasted_iota(jnp.int32, ba.shape, 1)
    beta = jax.nn.sigmoid(ba)
    g = -jnp.exp(al_ref[...]) * jax.nn.softplus(ba + dtb_ref[...])
    ri = lax.broadcasted_iota(jnp.int32, (T, T), 0)
    ci = lax.broadcasted_iota(jnp.int32, (T, T), 1)
    same = (ri // DN_CHUNK) == (ci // DN_CHUNK)
    tri_f = jnp.logical_and(same, ri >= ci).astype(f32)
    tri_b = jnp.logical_and(same, ri <= ci).astype(f32)
    gc_f = jnp.dot(tri_f, g, precision=HI, preferred_element_type=f32)
    gc_b = jnp.dot(tri_b, g, precision=HI, preferred_element_type=f32)
    nh = DN_HEADS
    gc = jnp.where(lane < 2 * nh + nh, gc_f, gc_b)
    gb = jnp.where(lane < 2 * nh, beta, jnp.where(lane < 4 * nh, gc, 0.0))
    gb_ref[0] = gb
    er = lax.broadcasted_iota(jnp.int32, (16, LANES), 0)
    ec = lax.broadcasted_iota(jnp.int32, (16, LANES), 1)
    sel = (er == ec).astype(f32)
    gbt = lax.dot_general(sel, gb, (((1,), (1,)), ((), ())), precision=HI, preferred_element_type=f32)
    for c in range(T // DN_CHUNK):
        gbt_ref[0, c] = gbt[:, c * DN_CHUNK:(c + 1) * DN_CHUNK]


def _dn_prepare(qkv, ba, conv_w, al_vec, dtb_vec, n_ctx_tiles):
    B, Lt, W = qkv.shape
    nt = Lt // TOK_TILE
    per8 = TOK_TILE // SUBLANES
    nblk8 = Lt // SUBLANES
    tok = lambda w: pl.BlockSpec((1, TOK_TILE, w), lambda b, t: (b, t, 0))
    body = functools.partial(_dn_prep_body, n_ctx_tiles=n_ctx_tiles, n_tiles=nt)
    return pl.pallas_call(
        body,
        out_shape=(jax.ShapeDtypeStruct((B, Lt, DN_W), f32),
                   jax.ShapeDtypeStruct((B, Lt, DN_W), f32),
                   jax.ShapeDtypeStruct((B, Lt, DN_W), f32),
                   jax.ShapeDtypeStruct((B, Lt, LANES), f32),
                   jax.ShapeDtypeStruct((B, Lt // DN_CHUNK, 16, DN_CHUNK), f32)),
        grid=(B, nt),
        in_specs=[tok(W),
                  pl.BlockSpec((1, SUBLANES, W), lambda b, t: (b, jnp.maximum(t * per8 - 1, 0), 0)),
                  pl.BlockSpec((1, SUBLANES, W), lambda b, t: (b, jnp.minimum((t + 1) * per8, nblk8 - 1), 0)),
                  tok(LANES),
                  _const_spec((SUBLANES, W)),
                  _const_spec((1, LANES)),
                  _const_spec((1, LANES))],
        out_specs=(tok(DN_W), tok(DN_W), tok(DN_W), tok(LANES),
                   pl.BlockSpec((1, TOK_TILE // DN_CHUNK, 16, DN_CHUNK), lambda b, t: (b, t, 0, 0))),
        scratch_shapes=[pltpu.VMEM((TOK_TILE + 2 * SUBLANES, W), f32)],
        compiler_params=_cparams(("arbitrary", "arbitrary")),
        name="dn_prepare",
    )(qkv, qkv, qkv, ba, conv_w, al_vec, dtb_vec)


DN_HALF = DN_CHUNK // 2
DN_PAIRS = DN_HEADS // 2
DN_STATE_BATCH = 4
DN_INTRA_CHUNKS = 4


def _diag_block_inverses(n_scr, revs):
    H = DN_HALF
    ng = H // SUBLANES
    ri = lax.broadcasted_iota(jnp.int32, (SUBLANES, LANES), 0)
    ci = lax.broadcasted_iota(jnp.int32, (SUBLANES, LANES), 1)
    eye = [(ri + SUBLANES * g == ci % H).astype(f32) for g in range(ng)]
    base = (ci // H) * H
    ts = [list(eye) for _ in revs]
    for step in range(H - 1):
        for p, rev in enumerate(revs):
            j = H - 1 - step if rev else step
            gj = j // SUBLANES
            idx = base + j
            row = ts[p][gj][j % SUBLANES:j % SUBLANES + 1, :]
            for g in (range(0, gj + 1) if rev else range(gj, ng)):
                col = jnp.take_along_axis(n_scr[p, SUBLANES * g:SUBLANES * (g + 1), :], idx, axis=1)
                ts[p][g] = ts[p][g] - col * row
    return [jnp.concatenate(t, axis=0) for t in ts]


def _split3_dot(a, b):
    a_hi = a.astype(bf16)
    a_lo = (a - a_hi.astype(f32)).astype(bf16)
    b_hi = b.astype(bf16)
    b_lo = (b - b_hi.astype(f32)).astype(bf16)
    return jnp.dot(jnp.concatenate([a_hi, a_hi, a_lo], axis=1), jnp.concatenate([b_hi, b_lo, b_hi], axis=0),
                   preferred_element_type=f32)


def _block_diag2(a, b):
    z = jnp.zeros_like(a)
    return jnp.concatenate([jnp.concatenate([a, z], axis=1), jnp.concatenate([z, b], axis=1)], axis=0)


def _dn_intra_body(q_ref, k_ref, v_ref, gb_ref, gbt_ref, uc_ref, wq_ref, qkkd_ref, np_scr, n_scr):
    C = DN_CHUNK
    H = DN_HALF
    ri = lax.broadcasted_iota(jnp.int32, (C, 2 * C), 0)
    ci = lax.broadcasted_iota(jnp.int32, (C, 2 * C), 1)
    cl = ci % C
    incl = (ri >= cl, ri <= cl)
    strict = (ri > cl, ri < cl)
    left = ci < C
    cih = lax.broadcasted_iota(jnp.int32, (H, 2 * C), 1)
    lo_h = (cih % C) < H
    left_h = cih < C
    zero_h = jnp.zeros((H, 2 * C), f32)
    last = (C - 1, 0)
    er = lax.broadcasted_iota(jnp.int32, (DN_DK, DN_DK), 0)
    ec = lax.broadcasted_iota(jnp.int32, (DN_DK, DN_DK), 1)
    eye_dk = (er == ec).astype(bf16)
    n_per = 2 * DN_PAIRS
    rows_of = [slice(cc * C, (cc + 1) * C) for cc in range(DN_INTRA_CHUNKS)]
    gbs = [gb_ref[0, rows_of[cc], :] for cc in range(DN_INTRA_CHUNKS)]

    for cc in range(DN_INTRA_CHUNKS):
        rows = rows_of[cc]
        gb = gbs[cc]
        for pr in range(DN_PAIRS):
            h0, h1 = 2 * pr, 2 * pr + 1
            s0 = slice(h0 * DN_DK, (h0 + 1) * DN_DK)
            s1 = slice(h1 * DN_DK, (h1 + 1) * DN_DK)
            k0 = k_ref[0, rows, s0].astype(bf16)
            k1 = k_ref[0, rows, s1].astype(bf16)
            kbd = _block_diag2(k0, k1)
            kk_p = lax.dot_general(jnp.concatenate([k0, k1], axis=1), kbd, (((1,), (1,)), ((), ())),
                                   preferred_element_type=f32)
            qcat = jnp.concatenate([q_ref[0, rows, s0], q_ref[0, rows, s1]], axis=1).astype(bf16)
            qk_p = lax.dot_general(qcat, kbd, (((1,), (1,)), ((), ())), preferred_element_type=f32)
            kt_p = lax.dot_general(eye_dk, jnp.concatenate([k0, k1], axis=0), (((1,), (1,)), ((), ())),
                                   preferred_element_type=f32)
            for d in range(2):
                bl0, bl1 = d * DN_HEADS + h0, d * DN_HEADS + h1
                gl0, gl1 = 2 * DN_HEADS + bl0, 2 * DN_HEADS + bl1
                beta_p = jnp.take_along_axis(gb, jnp.where(left, bl0, bl1), axis=1)
                gc_p = jnp.take_along_axis(gb, jnp.where(left, gl0, gl1), axis=1)
                gc_row = jnp.concatenate([gbt_ref[0, cc, gl0:gl0 + 1, :], gbt_ref[0, cc, gl1:gl1 + 1, :]], axis=1)
                decay = jnp.exp(jnp.where(incl[d], gc_p - gc_row, -jnp.inf))
                n_p = jnp.where(strict[d], (beta_p * kk_p) * decay, 0.0)
                p = cc * n_per + pr * 2 + d
                np_scr[p] = n_p
                n_scr[p] = jnp.where(lo_h, n_p[:H], n_p[H:])
                qkkd_ref[d, 0, cc, pr, 0:C, :] = jnp.where(incl[d], qk_p * decay, 0.0).astype(bf16)
                g_end = gc_p[last[d]:last[d] + 1, :]
                qkkd_ref[d, 0, cc, pr, C:C + DN_DK, :] = (kt_p * jnp.exp(g_end - gc_row)).astype(bf16)

    t_packs = _diag_block_inverses(n_scr, [False, True] * (DN_PAIRS * DN_INTRA_CHUNKS))

    def scaled(cc, hh, d):
        hs = slice(hh * DN_DK, (hh + 1) * DN_DK)
        rows = rows_of[cc]
        bl = d * DN_HEADS + hh
        full = jnp.zeros((C, LANES), jnp.int32)
        beta = jnp.take_along_axis(gbs[cc], full + bl, axis=1)
        egc = jnp.exp(jnp.take_along_axis(gbs[cc], full + (2 * DN_HEADS + bl), axis=1))
        wq_ref[d, 0, cc, C:2 * C, hs] = (q_ref[0, rows, hs] * egc).astype(bf16)
        return jnp.concatenate([v_ref[0, rows, hs] * beta, k_ref[0, rows, hs] * (beta * egc)], axis=1)

    packs = [(cc, pr, d) for cc in range(DN_INTRA_CHUNKS) for pr in range(DN_PAIRS) for d in range(2)]
    hi_h = jnp.logical_not(lo_h)
    right_h = jnp.logical_not(left_h)
    xs = []
    for cc, pr, d in packs:
        p = cc * n_per + pr * 2 + d
        tp = t_packs[p]
        if d == 0:
            nb = np_scr[p, H:C, :]
            r1 = jnp.concatenate([zero_h, jnp.where(jnp.logical_and(lo_h, left_h), nb, 0.0), zero_h,
                                  jnp.where(jnp.logical_and(lo_h, right_h), nb, 0.0)], axis=0)
        else:
            nt = np_scr[p, 0:H, :]
            r1 = jnp.concatenate([jnp.where(jnp.logical_and(hi_h, left_h), nt, 0.0), zero_h,
                                  jnp.where(jnp.logical_and(hi_h, right_h), nt, 0.0), zero_h], axis=0)
        xs.append(_split3_dot(tp, r1))
    ys = []
    for (cc, pr, d), x in zip(packs, xs):
        tp = t_packs[cc * n_per + pr * 2 + d]
        if d == 0:
            r2 = jnp.concatenate([jnp.where(cih < H, tp, 0.0), zero_h,
                                  jnp.where(jnp.logical_and(cih >= C, cih < C + H), tp, 0.0), zero_h], axis=0)
        else:
            r2 = jnp.concatenate([zero_h, jnp.where(jnp.logical_and(cih >= H, cih < C), tp, 0.0),
                                  zero_h, jnp.where(cih >= C + H, tp, 0.0)], axis=0)
        ys.append(_split3_dot(x, r2))
    for (cc, pr, d), y in zip(packs, ys):
        h0, h1 = 2 * pr, 2 * pr + 1
        s0 = slice(h0 * DN_DK, (h0 + 1) * DN_DK)
        s1 = slice(h1 * DN_DK, (h1 + 1) * DN_DK)
        rows = rows_of[cc]
        tp = t_packs[cc * n_per + pr * 2 + d]
        if d == 0:
            t_top = jnp.where(lo_h, tp, 0.0)
            t_bot = jnp.where(lo_h, -y, tp)
        else:
            t_top = jnp.where(lo_h, tp, -y)
            t_bot = jnp.where(lo_h, 0.0, tp)
        t_p = jnp.concatenate([t_top, t_bot], axis=0)
        sol = _split3_dot(t_p, _block_diag2(scaled(cc, h0, d), scaled(cc, h1, d)))
        uc_ref[d, 0, rows, s0] = sol[:, 0:DN_DV]
        wq_ref[d, 0, cc, 0:C, s0] = sol[:, DN_DV:2 * DN_DV].astype(bf16)
        uc_ref[d, 0, rows, s1] = sol[:, 2 * DN_DV:3 * DN_DV]
        wq_ref[d, 0, cc, 0:C, s1] = sol[:, 3 * DN_DV:4 * DN_DV].astype(bf16)


def _dn_intra(q, k, v, gb, gbt):
    B, Lt, W = q.shape
    C = DN_CHUNK
    nc = Lt // C
    ch = DN_INTRA_CHUNKS
    tok = lambda w: pl.BlockSpec((1, ch * C, w), lambda b, i: (b, i, 0))
    return pl.pallas_call(
        _dn_intra_body,
        out_shape=(jax.ShapeDtypeStruct((2, B, Lt, W), f32),
                   jax.ShapeDtypeStruct((2, B, nc, 2 * C, W), bf16),
                   jax.ShapeDtypeStruct((2, B, nc, DN_PAIRS, C + DN_DK, 2 * C), bf16)),
        grid=(B, nc // ch),
        in_specs=[tok(W), tok(W), tok(W), tok(LANES),
                  pl.BlockSpec((1, ch, 16, C), lambda b, i: (b, i, 0, 0))],
        out_specs=(pl.BlockSpec((2, 1, ch * C, W), lambda b, i: (0, b, i, 0)),
                   pl.BlockSpec((2, 1, ch, 2 * C, W), lambda b, i: (0, b, i, 0, 0)),
                   pl.BlockSpec((2, 1, ch, DN_PAIRS, C + DN_DK, 2 * C), lambda b, i: (0, b, i, 0, 0, 0))),
        scratch_shapes=[pltpu.VMEM((ch * 2 * DN_PAIRS, C, 2 * C), f32),
                        pltpu.VMEM((ch * 2 * DN_PAIRS, DN_HALF, 2 * C), f32)],
        compiler_params=_cparams(("arbitrary", "arbitrary")),
        name="dn_intra",
    )(q, k, v, gb, gbt)


def _dn_state_body(*refs, rev):
    if rev:
        uc_ref, wq_ref, qkkd_ref, gb_ref, of_ref, o_ref, s_scr = refs
    else:
        uc_ref, wq_ref, qkkd_ref, gb_ref, o_ref, s_scr = refs
    C = DN_CHUNK
    d = 1 if rev else 0
    last = 0 if rev else C - 1

    @pl.when(pl.program_id(1) == 0)
    def _():
        s_scr[...] = jnp.zeros_like(s_scr)

    for bb in range(DN_STATE_BATCH):
        for pr in range(DN_PAIRS):
            heads = (2 * pr, 2 * pr + 1)
            s_old, o_part, vbs = [], [], []
            for hh in heads:
                hs = slice(hh * DN_DK, (hh + 1) * DN_DK)
                s = s_scr[bb, hh]
                r = jnp.dot(wq_ref[0, bb, 0, :, hs], s.astype(bf16), preferred_element_type=f32)
                vbs.append((uc_ref[0, bb, :, hs] - r[0:C]).astype(bf16))
                o_part.append(r[C:2 * C])
                s_old.append(s)
            r2 = jnp.dot(qkkd_ref[0, bb, 0, pr], _block_diag2(vbs[0], vbs[1]), preferred_element_type=f32)
            for n, hh in enumerate(heads):
                hs = slice(hh * DN_DK, (hh + 1) * DN_DK)
                vs = slice(n * DN_DV, (n + 1) * DN_DV)
                gl = 2 * DN_HEADS + d * DN_HEADS + hh
                eg = jnp.exp(gb_ref[bb, last:last + 1, gl:gl + 1])
                s_scr[bb, hh] = s_old[n] * eg + r2[C:C + DN_DK, vs]
                o_h = o_part[n] + r2[0:C, vs]
                if rev:
                    o_h = o_h + of_ref[bb, :, hs]
                o_ref[bb, :, hs] = o_h


def _dn_state(uc, wq, qkkd, gb, n_ctx_chunks, rev, o_fwd=None):
    _, B, Lt, W = uc.shape
    C = DN_CHUNK
    nc = Lt // C
    nb = DN_STATE_BATCH
    d = 1 if rev else 0
    if rev:
        cidx = lambda i: jnp.where(i < n_ctx_chunks, n_ctx_chunks - 1 - i, nc - 1 - (i - n_ctx_chunks))
    else:
        cidx = lambda i: i
    tok = lambda w: pl.BlockSpec((nb, C, w), lambda b, i: (b, cidx(i), 0))
    in_specs = [pl.BlockSpec((1, nb, C, W), lambda b, i: (d, b, cidx(i), 0)),
                pl.BlockSpec((1, nb, 1, 2 * C, W), lambda b, i: (d, b, cidx(i), 0, 0)),
                pl.BlockSpec((1, nb, 1, DN_PAIRS, C + DN_DK, 2 * C), lambda b, i: (d, b, cidx(i), 0, 0, 0)),
                tok(LANES)]
    args = [uc, wq, qkkd, gb]
    if rev:
        in_specs.append(tok(W))
        args.append(o_fwd)
    return pl.pallas_call(
        functools.partial(_dn_state_body, rev=rev),
        out_shape=jax.ShapeDtypeStruct((B, Lt, W), f32),
        grid=(B // nb, nc),
        in_specs=in_specs,
        out_specs=tok(W),
        scratch_shapes=[pltpu.VMEM((nb, DN_HEADS, DN_DK, DN_DV), f32)],
        compiler_params=_cparams(("arbitrary", "arbitrary")),
        name="dn_state_bwd" if rev else "dn_state_fwd",
    )(*args)


def _attn_body(q_ref, kp_ref, kc_ref, kn_ref, kx_ref, vp_ref, vc_ref, vn_ref, vx_ref, sink_ref, o_ref,
               *, n_ctx_blocks, seq_len):
    Tq = AT_BLOCK
    G = AT_HEADS // AT_KV
    i = pl.program_id(1)
    is_ctx = i < n_ctx_blocks
    nloc = 3 * Tq
    qpos = (i - n_ctx_blocks) * Tq + lax.broadcasted_iota(jnp.int32, (Tq, nloc), 0)
    kpos = (i - n_ctx_blocks - 1) * Tq + lax.broadcasted_iota(jnp.int32, (Tq, nloc), 1)
    ok = jnp.logical_and(jnp.abs(qpos - kpos) <= WINDOW, jnp.logical_and(kpos >= 0, kpos < seq_len))
    ok = jnp.logical_and(ok, jnp.logical_not(is_ctx))
    k_all = jnp.concatenate([kp_ref[0], kc_ref[0], kn_ref[0], kx_ref[0]], axis=0).astype(bf16)
    v_all = jnp.concatenate([vp_ref[0], vc_ref[0], vn_ref[0], vx_ref[0]], axis=0)
    lane_q = lax.broadcasted_iota(jnp.int32, (Tq, LANES), 1)
    lane_v = lax.broadcasted_iota(jnp.int32, v_all.shape, 1)
    units = [(hk, g) for hk in range(AT_KV) for g in range(G)]
    nt = (((1,), (1,)), ((), ()))
    rhs = []
    for hk in range(AT_KV):
        mine = lane_v // AT_HD == hk
        rhs.append(jnp.concatenate([jnp.where(mine, v_all, 0.0), mine.astype(f32)], axis=1).astype(bf16))
    scores = []
    for hk, g in units:
        qs = jnp.where(lane_q // AT_HD == hk, q_ref[0, :, g * LANES:(g + 1) * LANES], 0.0).astype(bf16)
        s = lax.dot_general(qs, k_all, nt, preferred_element_type=f32)
        scores.append(jnp.concatenate([jnp.where(ok, s[:, :nloc], -jnp.inf), s[:, nloc:]], axis=1))
    sinks = [sink_ref[0:1, hk * G + g:hk * G + g + 1] for hk, g in units]
    maxes = [jnp.maximum(jnp.max(s, axis=-1, keepdims=True), sk) for s, sk in zip(scores, sinks)]
    probs = [jnp.exp(s - m).astype(bf16) for s, m in zip(scores, maxes)]
    outs = [jnp.dot(p, rhs[hk], preferred_element_type=f32) for p, (hk, g) in zip(probs, units)]
    for g in range(G):
        num = jnp.zeros((Tq, LANES), f32)
        den = jnp.zeros((Tq, LANES), f32)
        for u, (hk, gg) in enumerate(units):
            if gg == g:
                num = num + outs[u][:, 0:LANES]
                den = den + outs[u][:, LANES:2 * LANES] + jnp.where(lane_q // AT_HD == hk,
                                                                    jnp.exp(sinks[u] - maxes[u]), 0.0)
        o_ref[0, :, g * LANES:(g + 1) * LANES] = num / den


def _attention(aq, ak, av, sink_vec, n_ctx, seq_len):
    B, Lt, _ = aq.shape
    nb = Lt // AT_BLOCK
    n_ctx_blocks = n_ctx // AT_BLOCK
    blk = lambda w, f: pl.BlockSpec((1, AT_BLOCK, w), lambda b, i: (b, f(i), 0))
    lo, hi = n_ctx_blocks, nb - 1
    prev = lambda i: jnp.clip(i - 1, lo, hi)
    cur = lambda i: i
    nxt = lambda i: jnp.clip(i + 1, lo, hi)
    ctx_spec = pl.BlockSpec((1, n_ctx, AT_KW), lambda b, i: (b, 0, 0))
    body = functools.partial(_attn_body, n_ctx_blocks=n_ctx_blocks, seq_len=seq_len)
    return pl.pallas_call(
        body,
        out_shape=jax.ShapeDtypeStruct((B, Lt, AT_QW), f32),
        grid=(B, nb),
        in_specs=[blk(AT_QW, cur),
                  blk(AT_KW, prev), blk(AT_KW, cur), blk(AT_KW, nxt), ctx_spec,
                  blk(AT_KW, prev), blk(AT_KW, cur), blk(AT_KW, nxt), ctx_spec,
                  _const_spec((1, LANES))],
        out_specs=blk(AT_QW, cur),
        compiler_params=_cparams(("arbitrary", "arbitrary")),
        name="window_attn",
    )(aq, ak, ak, ak, ak, av, av, av, av, sink_vec)


def _merge_body(x_ref, mod_ref, g_ref, ys_ref, od_ref, z_ref, yc_ref,
                wg_ref, wglu_ref, ng_ref, wa_ref, wb_ref, wc_ref, wo_ref, o_ref):
    x = x_ref[0]
    h = _modnorm(x, g_ref[...], mod_ref[0, 1:2, :], mod_ref[0, 0:1, :]).astype(bf16)
    zs = jax.nn.gelu(ys_ref[...])
    ya = zs * jax.nn.sigmoid(_dot(zs, wglu_ref[...]))
    ng = ng_ref[...]
    parts = []
    for hh in range(DN_HEADS):
        hs = slice(hh * DN_DV, (hh + 1) * DN_DV)
        o = od_ref[0, :, hs]
        on = o * lax.rsqrt(jnp.mean(o * o, axis=-1, keepdims=True) + EPS) * ng
        parts.append(on * _silu(z_ref[0, :, hs]))
    yb = jnp.concatenate(parts, axis=1)
    yc = yc_ref[0]
    D = D_MODEL

    def gate(j):
        return jax.nn.sigmoid(jnp.dot(h, wg_ref[:, j * D:(j + 1) * D], preferred_element_type=f32))

    m = gate(0) * _dot(ya, wa_ref[...])
    m = m + gate(1) * _dot(yb, wb_ref[...])
    m = m + gate(2) * _dot(yc, wc_ref[...])
    mix = _dot(m, wo_ref[...])
    o_ref[0] = x + mod_ref[0, 2:3, :] * mix


def _merge(xcat, mods, g1, ys_tm, o_dn, dz, yc, w_gate, w_glu, dn_g, w_a, w_b, w_c, w_o, n_ctx_tiles):
    B, Lt, D = xcat.shape
    nt = Lt // TOK_TILE
    tok = lambda w: pl.BlockSpec((1, TOK_TILE, w), lambda b, t: (b, t, 0))
    return pl.pallas_call(
        _merge_body,
        out_shape=jax.ShapeDtypeStruct((B, Lt, D), f32),
        grid=(B, nt),
        in_specs=[tok(D),
                  pl.BlockSpec((1, N_MOD, D), _mod_index(n_ctx_tiles, B)),
                  _const_spec((1, D)),
                  pl.BlockSpec((TOK_TILE, SSM_WIDTH), lambda b, t: (t, b)),
                  tok(DN_W), tok(DN_W), tok(AT_QW),
                  _const_spec(w_gate.shape), _const_spec(w_glu.shape), _const_spec((1, DN_DV)),
                  _const_spec(w_a.shape), _const_spec(w_b.shape), _const_spec(w_c.shape),
                  _const_spec(w_o.shape)],
        out_specs=tok(D),
        compiler_params=_cparams(("arbitrary", "arbitrary")),
        name="merge_out",
    )(xcat, mods, g1, ys_tm, o_dn, dz, yc, w_gate, w_glu, dn_g, w_a, w_b, w_c, w_o)


FF_CHUNK = 1024


def _ffn_body(x_ref, mod_ref, g_ref, w1_ref, w2_ref, fg_ref, o_ref):
    x = x_ref[0]
    h = _modnorm(x, g_ref[...], mod_ref[0, 4:5, :], mod_ref[0, 3:4, :]).astype(bf16)
    acc = jnp.zeros(x.shape, f32)
    for j in range(D_FF // FF_CHUNK):
        a = jnp.dot(h, w1_ref[:, j * FF_CHUNK:(j + 1) * FF_CHUNK], preferred_element_type=f32)
        a = jnp.square(jnp.maximum(a, 0.0)).astype(bf16)
        acc = acc + jnp.dot(a, w2_ref[j * FF_CHUNK:(j + 1) * FF_CHUNK, :], preferred_element_type=f32)
    y = x + mod_ref[0, 5:6, :] * acc
    if fg_ref is not None:
        y = y * lax.rsqrt(jnp.mean(y * y, axis=-1, keepdims=True) + EPS) * fg_ref[...]
    o_ref[0] = y


def _ffn_mid_body(x_ref, mod_ref, g_ref, w1_ref, w2_ref, o_ref):
    _ffn_body(x_ref, mod_ref, g_ref, w1_ref, w2_ref, None, o_ref)


def _ffn(xcat, mods, g2, w1, w2, n_ctx_tiles, final_g=None):
    B, Lt, D = xcat.shape
    nt = Lt // TOK_TILE
    tok = pl.BlockSpec((1, TOK_TILE, D), lambda b, t: (b, t, 0))
    weights = [_const_spec((1, D)), _const_spec(w1.shape), _const_spec(w2.shape)]
    if final_g is None:
        return pl.pallas_call(
            _ffn_mid_body,
            out_shape=jax.ShapeDtypeStruct((B, Lt, D), f32),
            grid=(B, nt),
            in_specs=[tok, pl.BlockSpec((1, N_MOD, D), _mod_index(n_ctx_tiles, B))] + weights,
            out_specs=tok,
            compiler_params=_cparams(("arbitrary", "arbitrary")),
            name="ffn",
        )(xcat, mods, g2, w1, w2)
    return pl.pallas_call(
        _ffn_body,
        out_shape=jax.ShapeDtypeStruct((B, Lt - n_ctx_tiles * TOK_TILE, D), f32),
        grid=(B, nt - n_ctx_tiles),
        in_specs=[pl.BlockSpec((1, TOK_TILE, D), lambda b, t: (b, t + n_ctx_tiles, 0)),
                  pl.BlockSpec((1, N_MOD, D), lambda b, t: (b, 0, 0))] + weights + [_const_spec((1, D))],
        out_specs=tok,
        compiler_params=_cparams(("arbitrary", "arbitrary")),
        name="ffn_final",
    )(xcat, mods, g2, w1, w2, final_g)


def _rope_tables(n_ctx, seq_len):
    n = AT_HD // 4
    inv_freq = ROPE_BASE ** (-jnp.arange(n, dtype=f32) / n)
    pos = jnp.arange(seq_len, dtype=jnp.int32)
    rows = (pos // GRID_W).astype(f32)[:, None] * inv_freq[None, :]
    cols = (pos % GRID_W).astype(f32)[:, None] * inv_freq[None, :]
    cos = jnp.concatenate([jnp.cos(rows)] * 2 + [jnp.cos(cols)] * 2, axis=1)
    sin = jnp.concatenate([-jnp.sin(rows), jnp.sin(rows), -jnp.sin(cols), jnp.sin(cols)], axis=1)
    cos = jnp.concatenate([jnp.ones((n_ctx, AT_HD), f32), cos], axis=0)
    sin = jnp.concatenate([jnp.zeros((n_ctx, AT_HD), f32), sin], axis=0)
    return jnp.tile(cos, (1, LANES // AT_HD)), jnp.tile(sin, (1, LANES // AT_HD))


def _attn_head_perm():
    G = AT_HEADS // AT_KV
    n = jnp.arange(AT_QW)
    g, hk, e = n // LANES, (n % LANES) // AT_HD, n % AT_HD
    return (hk * G + g) * AT_HD + e


def _split_w_in(w_in):
    o = SSM_WIDTH + 4 * DN_W
    w_ba = w_in[:, o:o + 4 * DN_HEADS]
    o2 = o + 4 * DN_HEADS
    w_att = w_in[:, o2:o2 + AT_QW + 2 * AT_KW]
    w_att = jnp.concatenate([w_att[:, :AT_QW][:, _attn_head_perm()], w_att[:, AT_QW:]], axis=1)
    w_gate = w_in[:, o2 + AT_QW + 2 * AT_KW:]
    w_main = jnp.concatenate([w_in[:, :o], w_att], axis=1).astype(bf16)
    w_ba = jnp.pad(w_ba, ((0, 0), (0, LANES - 4 * DN_HEADS))).astype(bf16)
    return w_main, w_ba, w_gate.astype(bf16)


def kernel(x, c, ctx, c_ctx, norm1_g, norm2_g, w_mod, b_mod, w_in, ssm_lam_re, ssm_lam_im, ssm_log_dt, ssm_b_re, ssm_b_im, ssm_c_re, ssm_c_im, ssm_d, ssm_w_glu, dn_conv_w, dn_a_log, dn_dt_bias, dn_norm_g, attn_sink, w_branch_a, w_branch_b, w_branch_c, w_out, w_ff1, w_ff2, final_norm_g):
    B, L, D = x.shape
    Lc = ctx.shape[1]
    depth = w_in.shape[0]
    assert B == SUBLANES and D == D_MODEL
    assert Lc % TOK_TILE == 0 and L % TOK_TILE == 0
    Lt = Lc + L
    n_ctx_tiles = Lc // TOK_TILE

    cond = jnp.concatenate([c, c_ctx[None, :], jnp.zeros((16 - B - 1, D), f32)], axis=0)
    mods_all = _modulation(cond, w_mod, b_mod).reshape(depth, 16, N_MOD, D)
    cos_t, sin_t = _rope_tables(Lc, L)
    xcat = jnp.concatenate([ctx, x], axis=1)

    for layer in range(depth):
        mods = mods_all[layer]
        g1 = norm1_g[layer].reshape(1, D)
        g2 = norm2_g[layer].reshape(1, D)
        w_main, w_ba, w_gate = _split_w_in(w_in[layer])

        u_tm, dqkv, dz, dba, aq, ak, av = _inproj(xcat, mods, g1, w_main, w_ba, cos_t, sin_t, n_ctx_tiles)

        a_re, a_im, bbt_re, bbt_im = _s5_discretize(ssm_lam_re[layer], ssm_lam_im[layer], ssm_log_dt[layer],
                                                    ssm_b_re[layer], ssm_b_im[layer])
        bmat, cmat, avec = _s5_matrices(a_re, a_im, bbt_re, bbt_im, ssm_c_re[layer], ssm_c_im[layer])
        u_rows = u_tm.reshape(Lt * B, SSM_WIDTH)
        y_f = _s5_scan(u_rows, bmat, cmat, avec, Lc, rev=False)
        y_s5 = _s5_scan(u_rows, bmat, cmat, avec, Lc, rev=True, yf=y_f,
                        d_skip=ssm_d[layer].reshape(1, SSM_WIDTH))
        ys_tm = y_s5.reshape(Lt, B * SSM_WIDTH)

        conv_w = jnp.pad(dn_conv_w[layer], ((0, SUBLANES - DN_CONV), (0, 0)))
        pad_l = 2 * DN_HEADS
        al_vec = jnp.pad(dn_a_log[layer].reshape(1, 2 * DN_HEADS), ((0, 0), (pad_l, LANES - 2 * pad_l)))
        dtb_vec = jnp.pad(dn_dt_bias[layer].reshape(1, 2 * DN_HEADS), ((0, 0), (pad_l, LANES - 2 * pad_l)))
        qn, kn, vs, gb, gbt = _dn_prepare(dqkv, dba, conv_w, al_vec, dtb_vec, n_ctx_tiles)
        n_ctx_chunks = Lc // DN_CHUNK
        uc, wq, qkkd = _dn_intra(qn, kn, vs, gb, gbt)
        o_f = _dn_state(uc, wq, qkkd, gb, n_ctx_chunks, rev=False)
        o_dn = _dn_state(uc, wq, qkkd, gb, n_ctx_chunks, rev=True, o_fwd=o_f)

        sink_vec = jnp.pad(attn_sink[layer].reshape(1, AT_HEADS), ((0, 0), (0, LANES - AT_HEADS)))
        yc = _attention(aq, ak, av, sink_vec, Lc, L)

        x1 = _merge(xcat, mods, g1, ys_tm, o_dn, dz, yc, w_gate, ssm_w_glu[layer].astype(bf16),
                    dn_norm_g[layer].reshape(1, DN_DV), w_branch_a[layer].astype(bf16),
                    w_branch_b[layer].astype(bf16), w_branch_c[layer][_attn_head_perm(), :].astype(bf16),
                    w_out[layer].astype(bf16), n_ctx_tiles)
        final_g = final_norm_g.reshape(1, D) if layer == depth - 1 else None
        xcat = _ffn(x1, mods, g2, w_ff1[layer].astype(bf16), w_ff2[layer].astype(bf16), n_ctx_tiles, final_g)

    return xcat
```

```python
import functools
import math

import jax
import jax.numpy as jnp
from jax import lax
from jax.experimental import pallas as pl
from jax.experimental.pallas import tpu as pltpu

f32 = jnp.float32
bf16 = jnp.bfloat16
HI = lax.Precision.HIGHEST

D_MODEL = 1024
GRID_W = 64
EPS = 1e-6
SSM_WIDTH = D_MODEL // 2
SSM_GROUP = 16
SSM_GROUPS = SSM_WIDTH // SSM_GROUP
SSM_STATE = 64
DN_HEADS = 4
DN_DK = 128
DN_DV = 128
DN_CONV = 5
DN_CHUNK = 64
AT_HEADS = 8
AT_KV = 2
AT_HD = 64
WINDOW = 128
AT_BLOCK = 128
AT_UNIT_GROUPS = 1
ROPE_BASE = 10000.0
D_FF = 4 * D_MODEL
N_MOD = 6
DN_W = DN_HEADS * DN_DK
AT_QW = AT_HEADS * AT_HD
AT_KW = AT_KV * AT_HD

SUBLANES = 8
LANES = 128
TOK_TILE = 256
S5_STEPS = 128
S5_COLBLK = 128
S5_NBLK = SSM_WIDTH // S5_COLBLK
S5_SBLK = (S5_COLBLK // SSM_GROUP) * SSM_STATE
VMEM_LIMIT = 56 * 1024 * 1024


def _cparams(sem):
    return pltpu.CompilerParams(dimension_semantics=sem, vmem_limit_bytes=VMEM_LIMIT)


def _const_spec(shape):
    nd = len(shape)
    return pl.BlockSpec(shape, lambda *_: (0,) * nd, pipeline_mode=pl.Buffered(1))


def _dot(a, b):
    return jnp.dot(a.astype(bf16), b.astype(bf16), preferred_element_type=f32)


def _dot_nt(a, b):
    return lax.dot_general(a.astype(bf16), b.astype(bf16), (((1,), (1,)), ((), ())),
                           preferred_element_type=f32)


def _dot_tn(a, b):
    return lax.dot_general(a.astype(bf16), b.astype(bf16), (((0,), (0,)), ((), ())),
                           preferred_element_type=f32)


def _modnorm(x, g, scale, shift):
    y = x * lax.rsqrt(jnp.mean(x * x, axis=-1, keepdims=True) + EPS)
    return (y * g) * (1.0 + scale) + shift


def _silu(x):
    return x * jax.nn.sigmoid(x)


def _mod_body(c_ref, w_ref, b_ref, o_ref):
    s = _silu(c_ref[...])
    o_ref[0] = jnp.dot(s, w_ref[0], precision=HI, preferred_element_type=f32) + b_ref[0]


def _modulation(cond, w_mod, b_mod):
    depth = w_mod.shape[0]
    nblk = (N_MOD * D_MODEL) // D_MODEL
    return pl.pallas_call(
        _mod_body,
        out_shape=jax.ShapeDtypeStruct((depth, 16, N_MOD * D_MODEL), f32),
        grid=(depth, nblk),
        in_specs=[pl.BlockSpec((16, D_MODEL), lambda l, j: (0, 0)),
                  pl.BlockSpec((1, D_MODEL, D_MODEL), lambda l, j: (l, 0, j)),
                  pl.BlockSpec((1, 1, D_MODEL), lambda l, j: (l, 0, j))],
        out_specs=pl.BlockSpec((1, 16, D_MODEL), lambda l, j: (l, 0, j)),
        compiler_params=_cparams(("arbitrary", "arbitrary")),
        name="adaln_mod",
    )(cond, w_mod, b_mod.reshape(depth, 1, N_MOD * D_MODEL))


def _rope(x, cos, sin):
    n = x.shape[-1]
    lane = lax.broadcasted_iota(jnp.int32, x.shape, 1)
    partner = jnp.where((lane % 32) < 16, pltpu.roll(x, n - 16, 1), pltpu.roll(x, 16, 1))
    return x * cos + partner * sin


def _inproj_body(x_ref, mod_ref, g_ref, wm_ref, wba_ref, cos_ref, sin_ref,
                 u_ref, qkv_ref, z_ref, ba_ref, aq_ref, ak_ref, av_ref):
    h = _modnorm(x_ref[0], g_ref[...], mod_ref[0, 1:2, :], mod_ref[0, 0:1, :]).astype(bf16)

    def proj(lo, hi):
        return jnp.dot(h, wm_ref[:, lo:hi], preferred_element_type=f32)

    o = 0
    u_ref[...] = proj(o, o + SSM_WIDTH)
    o += SSM_WIDTH
    qkv_ref[0] = proj(o, o + 3 * DN_W)
    o += 3 * DN_W
    z_ref[0] = proj(o, o + DN_W)
    o += DN_W
    cos = cos_ref[...]
    sin = sin_ref[...]
    reps = AT_QW // LANES
    aq_ref[0] = _rope(proj(o, o + AT_QW), jnp.concatenate([cos] * reps, axis=1),
                      jnp.concatenate([sin] * reps, axis=1)) * (AT_HD ** -0.5)
    o += AT_QW
    ak_ref[0] = _rope(proj(o, o + AT_KW), cos, sin)
    o += AT_KW
    av_ref[0] = proj(o, o + AT_KW)
    ba_ref[0] = jnp.dot(h, wba_ref[...], preferred_element_type=f32)


def _mod_index(n_ctx_tiles, ctx_row):
    return lambda b, t: (jnp.where(t < n_ctx_tiles, ctx_row, b), 0, 0)


def _inproj(xcat, mods, g1, w_main, w_ba, cos_t, sin_t, n_ctx_tiles):
    B, Lt, D = xcat.shape
    nt = Lt // TOK_TILE
    wm = w_main.shape[1]
    tok = lambda w: pl.BlockSpec((1, TOK_TILE, w), lambda b, t: (b, t, 0))
    out_shape = (
        jax.ShapeDtypeStruct((Lt, B * SSM_WIDTH), f32),
        jax.ShapeDtypeStruct((B, Lt, 3 * DN_W), f32),
        jax.ShapeDtypeStruct((B, Lt, DN_W), f32),
        jax.ShapeDtypeStruct((B, Lt, LANES), f32),
        jax.ShapeDtypeStruct((B, Lt, AT_QW), f32),
        jax.ShapeDtypeStruct((B, Lt, AT_KW), f32),
        jax.ShapeDtypeStruct((B, Lt, AT_KW), f32),
    )
    return pl.pallas_call(
        _inproj_body,
        out_shape=out_shape,
        grid=(B, nt),
        in_specs=[tok(D),
                  pl.BlockSpec((1, N_MOD, D), _mod_index(n_ctx_tiles, B)),
                  _const_spec((1, D)),
                  _const_spec((D, wm)),
                  _const_spec((D, LANES)),
                  pl.BlockSpec((TOK_TILE, LANES), lambda b, t: (t, 0)),
                  pl.BlockSpec((TOK_TILE, LANES), lambda b, t: (t, 0))],
        out_specs=(pl.BlockSpec((TOK_TILE, SSM_WIDTH), lambda b, t: (t, b)),
                   tok(3 * DN_W), tok(DN_W), tok(LANES), tok(AT_QW), tok(AT_KW), tok(AT_KW)),
        compiler_params=_cparams(("arbitrary", "arbitrary")),
        name="in_proj",
    )(xcat, mods, g1, w_main, w_ba, cos_t, sin_t)


def _s5disc_body(lre_ref, lim_ref, ldt_ref, bre_ref, bim_ref, are_ref, aim_ref, bbre_ref, bbim_ref):
    lr = lre_ref[0]
    li = lim_ref[0]
    dt = jnp.exp(ldt_ref[0])
    mag = jnp.exp(lr * dt)
    a_re = mag * jnp.cos(li * dt)
    a_im = mag * jnp.sin(li * dt)
    den = lr * lr + li * li
    f_re = ((a_re - 1.0) * lr + a_im * li) / den
    f_im = (a_im * lr - (a_re - 1.0) * li) / den
    are_ref[0] = a_re
    aim_ref[0] = a_im
    b_re = bre_ref[0]
    b_im = bim_ref[0]
    bbre_ref[0] = f_re * b_re - f_im * b_im
    bbim_ref[0] = f_re * b_im + f_im * b_re


def _s5_discretize(lam_re, lam_im, log_dt, b_re, b_im):
    n = 2 * SSM_GROUPS
    P, H = SSM_STATE, SSM_GROUP
    vec = pl.BlockSpec((1, 1, P), lambda i: (i, 0, 0))
    mat = pl.BlockSpec((1, H, P), lambda i: (i, 0, 0))
    return pl.pallas_call(
        _s5disc_body,
        out_shape=(jax.ShapeDtypeStruct((n, 1, P), f32), jax.ShapeDtypeStruct((n, 1, P), f32),
                   jax.ShapeDtypeStruct((n, H, P), f32), jax.ShapeDtypeStruct((n, H, P), f32)),
        grid=(n,),
        in_specs=[vec, vec, pl.BlockSpec((1, 1, 1), lambda i: (i, 0, 0)), mat, mat],
        out_specs=(vec, vec, mat, mat),
        compiler_params=_cparams(("arbitrary",)),
        name="s5_discretize",
    )(lam_re.reshape(n, 1, P), lam_im.reshape(n, 1, P), log_dt.reshape(n, 1, 1),
      jnp.swapaxes(b_re, -1, -2).reshape(n, H, P), jnp.swapaxes(b_im, -1, -2).reshape(n, H, P))


def _s5_matrices(a_re, a_im, bbt_re, bbt_im, c_re, c_im):
    gb = S5_COLBLK // SSM_GROUP
    P, H = SSM_STATE, SSM_GROUP
    eye = jnp.eye(gb, dtype=f32)

    def bblk(t):
        t = t.reshape(2, S5_NBLK, gb, H, P)
        return jnp.einsum('dcghp,gk->dcghkp', t, eye).reshape(2, S5_NBLK, gb * H, gb * P)

    def cblk(t):
        t = t.reshape(2, S5_NBLK, gb, H, P)
        return jnp.einsum('dcghp,gk->dcgpkh', t, eye).reshape(2, S5_NBLK, gb * P, gb * H)

    bmat = jnp.concatenate([bblk(bbt_re), bblk(bbt_im)], axis=-1).astype(bf16)
    cmat = jnp.concatenate([cblk(c_re.astype(f32)), -cblk(c_im.astype(f32))], axis=-2).astype(bf16)
    avec = jnp.stack([a_re.reshape(2, S5_NBLK, gb * P), a_im.reshape(2, S5_NBLK, gb * P)], axis=2)
    return bmat, cmat, avec


def _s5_scan_body(*refs, rev):
    if rev:
        u_ref, bm_ref, cm_ref, a_ref, yf_ref, d_ref, y_ref, bu_scr, st_scr = refs
    else:
        u_ref, bm_ref, cm_ref, a_ref, y_ref, bu_scr, st_scr = refs
    B = SUBLANES
    nsb = S5_SBLK

    @pl.when(pl.program_id(0) == 0)
    def _():
        st_scr[...] = jnp.zeros_like(st_scr)

    def tile_rows(t):
        return pl.ds(((S5_STEPS - 1 - t) if rev else t) * B, B)

    for cb in range(S5_NBLK):
        cols = slice(cb * S5_COLBLK, (cb + 1) * S5_COLBLK)
        if rev:
            u_blk = jnp.concatenate([u_ref[tile_rows(t), cols] for t in range(S5_STEPS)], axis=0)
        else:
            u_blk = u_ref[:, cols]
        bu_scr[cb] = jnp.dot(u_blk.astype(bf16), bm_ref[0, cb], preferred_element_type=f32)
        are = jnp.broadcast_to(a_ref[0, cb, 0:1, :], (B, nsb))
        aim = jnp.broadcast_to(a_ref[0, cb, 1:2, :], (B, nsb))
        sre = st_scr[cb, :, 0:nsb]
        sim = st_scr[cb, :, nsb:2 * nsb]
        for t in range(S5_STEPS):
            rows = pl.ds(t * B, B)
            nre = are * sre - aim * sim + bu_scr[cb, rows, 0:nsb]
            nim = are * sim + aim * sre + bu_scr[cb, rows, nsb:2 * nsb]
            bu_scr[cb, rows, 0:nsb] = nre
            bu_scr[cb, rows, nsb:2 * nsb] = nim
            sre, sim = nre, nim
        st_scr[cb, :, 0:nsb] = sre
        st_scr[cb, :, nsb:2 * nsb] = sim
        y = jnp.dot(bu_scr[cb].astype(bf16), cm_ref[0, cb], preferred_element_type=f32)
        if rev:
            for t in range(S5_STEPS):
                rows = tile_rows(t)
                y_ref[rows, cols] = (y[t * B:(t + 1) * B, :] + yf_ref[rows, cols]
                                     + d_ref[:, cols] * u_ref[rows, cols])
        else:
            y_ref[:, cols] = y


def _s5_tile_index(n_ctx_tiles, n_tiles, rev):
    if not rev:
        return lambda i: (i, 0)
    return lambda i: (jnp.where(i < n_ctx_tiles, n_ctx_tiles - 1 - i, n_tiles - 1 - (i - n_ctx_tiles)), 0)


def _s5_scan(u_rows, bmat, cmat, avec, n_ctx_steps, rev, yf=None, d_skip=None):
    n_rows = u_rows.shape[0]
    rows = S5_STEPS * SUBLANES
    n_tiles = n_rows // rows
    n_ctx_tiles = n_ctx_steps // S5_STEPS
    idx = _s5_tile_index(n_ctx_tiles, n_tiles, rev)
    d = 1 if rev else 0
    tile = pl.BlockSpec((rows, SSM_WIDTH), idx)
    in_specs = [tile,
                pl.BlockSpec((1, S5_NBLK, S5_COLBLK, 2 * S5_SBLK), lambda i: (d, 0, 0, 0), pipeline_mode=pl.Buffered(1)),
                pl.BlockSpec((1, S5_NBLK, 2 * S5_SBLK, S5_COLBLK), lambda i: (d, 0, 0, 0), pipeline_mode=pl.Buffered(1)),
                pl.BlockSpec((1, S5_NBLK, 2, S5_SBLK), lambda i: (d, 0, 0, 0), pipeline_mode=pl.Buffered(1))]
    args = [u_rows, bmat, cmat, avec]
    if rev:
        in_specs += [tile, _const_spec((1, SSM_WIDTH))]
        args += [yf, d_skip]
    return pl.pallas_call(
        functools.partial(_s5_scan_body, rev=rev),
        out_shape=jax.ShapeDtypeStruct((n_rows, SSM_WIDTH), f32),
        grid=(n_tiles,),
        in_specs=in_specs,
        out_specs=tile,
        scratch_shapes=[pltpu.VMEM((S5_NBLK, rows, 2 * S5_SBLK), f32),
                        pltpu.VMEM((S5_NBLK, SUBLANES, 2 * S5_SBLK), f32)],
        compiler_params=_cparams(("arbitrary",)),
        name="s5_scan_bwd" if rev else "s5_scan_fwd",
    )(*args)


def _dn_prep_body(qkv_ref, prev_ref, next_ref, ba_ref, cw_ref, al_ref, dtb_ref,
                  q_ref, k_ref, v_ref, gb_ref, gbt_ref, ext_scr, *, n_ctx_tiles, n_tiles):
    T = TOK_TILE
    t = pl.program_id(1)
    prev_ok = jnp.logical_and(t != 0, t != n_ctx_tiles)
    next_ok = jnp.logical_and(t != n_ctx_tiles - 1, t != n_tiles - 1)
    ext_scr[0:SUBLANES, :] = jnp.where(prev_ok, prev_ref[0], 0.0)
    ext_scr[SUBLANES:SUBLANES + T, :] = qkv_ref[0]
    ext_scr[SUBLANES + T:2 * SUBLANES + T, :] = jnp.where(next_ok, next_ref[0], 0.0)
    half = DN_CONV // 2
    acc = cw_ref[0:1, :] * ext_scr[pl.ds(SUBLANES - half, T), :]
    for kk in range(1, DN_CONV):
        acc = acc + cw_ref[kk:kk + 1, :] * ext_scr[pl.ds(SUBLANES - half + kk, T), :]
    act = _silu(acc)
    for hh in range(DN_HEADS):
        qs = act[:, hh * DN_DK:(hh + 1) * DN_DK]
        ks = act[:, DN_W + hh * DN_DK:DN_W + (hh + 1) * DN_DK]
        q_ref[0, :, hh * DN_DK:(hh + 1) * DN_DK] = qs * (lax.rsqrt(jnp.sum(qs * qs, axis=-1, keepdims=True) + EPS)
                                                       * (DN_DK ** -0.5))
        k_ref[0, :, hh * DN_DK:(hh + 1) * DN_DK] = ks * lax.rsqrt(jnp.sum(ks * ks, axis=-1, keepdims=True) + EPS)
    v_ref[0] = act[:, 2 * DN_W:3 * DN_W]

    ba = ba_ref[0]
    lane = lax.broadcasted_iota(jnp.int32, ba.shape, 1)
    beta = jax.nn.sigmoid(ba)
    g = -jnp.exp(al_ref[...]) * jax.nn.softplus(ba + dtb_ref[...])
    ri = lax.broadcasted_iota(jnp.int32, (T, T), 0)
    ci = lax.broadcasted_iota(jnp.int32, (T, T), 1)
    same = (ri // DN_CHUNK) == (ci // DN_CHUNK)
    tri_f = jnp.logical_and(same, ri >= ci).astype(f32)
    tri_b = jnp.logical_and(same, ri <= ci).astype(f32)
    gc_f = jnp.dot(tri_f, g, precision=HI, preferred_element_type=f32)
    gc_b = jnp.dot(tri_b, g, precision=HI, preferred_element_type=f32)
    nh = DN_HEADS
    gc = jnp.where(lane < 2 * nh + nh, gc_f, gc_b)
    gb = jnp.where(lane < 2 * nh, beta, jnp.where(lane < 4 * nh, gc, 0.0))
    gb_ref[0] = gb
    er = lax.broadcasted_iota(jnp.int32, (16, LANES), 0)
    ec = lax.broadcasted_iota(jnp.int32, (16, LANES), 1)
    sel = (er == ec).astype(f32)
    gbt = lax.dot_general(sel, gb, (((1,), (1,)), ((), ())), precision=HI, preferred_element_type=f32)
    for c in range(T // DN_CHUNK):
        gbt_ref[0, c] = gbt[:, c * DN_CHUNK:(c + 1) * DN_CHUNK]


def _dn_prepare(qkv, ba, conv_w, al_vec, dtb_vec, n_ctx_tiles):
    B, Lt, W = qkv.shape
    nt = Lt // TOK_TILE
    per8 = TOK_TILE // SUBLANES
    nblk8 = Lt // SUBLANES
    tok = lambda w: pl.BlockSpec((1, TOK_TILE, w), lambda b, t: (b, t, 0))
    body = functools.partial(_dn_prep_body, n_ctx_tiles=n_ctx_tiles, n_tiles=nt)
    return pl.pallas_call(
        body,
        out_shape=(jax.ShapeDtypeStruct((B, Lt, DN_W), f32),
                   jax.ShapeDtypeStruct((B, Lt, DN_W), f32),
                   jax.ShapeDtypeStruct((B, Lt, DN_W), f32),
                   jax.ShapeDtypeStruct((B, Lt, LANES), f32),
                   jax.ShapeDtypeStruct((B, Lt // DN_CHUNK, 16, DN_CHUNK), f32)),
        grid=(B, nt),
        in_specs=[tok(W),
                  pl.BlockSpec((1, SUBLANES, W), lambda b, t: (b, jnp.maximum(t * per8 - 1, 0), 0)),
                  pl.BlockSpec((1, SUBLANES, W), lambda b, t: (b, jnp.minimum((t + 1) * per8, nblk8 - 1), 0)),
                  tok(LANES),
                  _const_spec((SUBLANES, W)),
                  _const_spec((1, LANES)),
                  _const_spec((1, LANES))],
        out_specs=(tok(DN_W), tok(DN_W), tok(DN_W), tok(LANES),
                   pl.BlockSpec((1, TOK_TILE // DN_CHUNK, 16, DN_CHUNK), lambda b, t: (b, t, 0, 0))),
        scratch_shapes=[pltpu.VMEM((TOK_TILE + 2 * SUBLANES, W), f32)],
        compiler_params=_cparams(("arbitrary", "arbitrary")),
        name="dn_prepare",
    )(qkv, qkv, qkv, ba, conv_w, al_vec, dtb_vec)


DN_HALF = DN_CHUNK // 2
DN_PAIRS = DN_HEADS // 2
DN_STATE_BATCH = 8
DN_INTRA_CHUNKS = 4


def _diag_block_inverses(n_scr, revs):
    H = DN_HALF
    ng = H // SUBLANES
    ri = lax.broadcasted_iota(jnp.int32, (SUBLANES, LANES), 0)
    ci = lax.broadcasted_iota(jnp.int32, (SUBLANES, LANES), 1)
    eye = [(ri + SUBLANES * g == ci % H).astype(f32) for g in range(ng)]
    base = (ci // H) * H
    ts = [list(eye) for _ in revs]
    for step in range(H - 1):
        for p, rev in enumerate(revs):
            j = H - 1 - step if rev else step
            gj = j // SUBLANES
            idx = base + j
            row = ts[p][gj][j % SUBLANES:j % SUBLANES + 1, :]
            for g in (range(0, gj + 1) if rev else range(gj, ng)):
                col = jnp.take_along_axis(n_scr[p, SUBLANES * g:SUBLANES * (g + 1), :], idx, axis=1)
                ts[p][g] = ts[p][g] - col * row
    return [jnp.concatenate(t, axis=0) for t in ts]


def _split3_dot(a, b):
    a_hi = a.astype(bf16)
    a_lo = (a - a_hi.astype(f32)).astype(bf16)
    b_hi = b.astype(bf16)
    b_lo = (b - b_hi.astype(f32)).astype(bf16)
    return jnp.dot(jnp.concatenate([a_hi, a_hi, a_lo], axis=1), jnp.concatenate([b_hi, b_lo, b_hi], axis=0),
                   preferred_element_type=f32)


def _block_diag2(a, b):
    z = jnp.zeros_like(a)
    return jnp.concatenate([jnp.concatenate([a, z], axis=1), jnp.concatenate([z, b], axis=1)], axis=0)


def _dn_intra_body(q_ref, k_ref, v_ref, gb_ref, gbt_ref, uc_ref, wq_ref, qkkd_ref, np_scr, n_scr):
    C = DN_CHUNK
    H = DN_HALF
    ri = lax.broadcasted_iota(jnp.int32, (C, 2 * C), 0)
    ci = lax.broadcasted_iota(jnp.int32, (C, 2 * C), 1)
    cl = ci % C
    incl = (ri >= cl, ri <= cl)
    strict = (ri > cl, ri < cl)
    left = ci < C
    cih = lax.broadcasted_iota(jnp.int32, (H, 2 * C), 1)
    lo_h = (cih % C) < H
    left_h = cih < C
    zero_h = jnp.zeros((H, 2 * C), f32)
    last = (C - 1, 0)
    er = lax.broadcasted_iota(jnp.int32, (DN_DK, DN_DK), 0)
    ec = lax.broadcasted_iota(jnp.int32, (DN_DK, DN_DK), 1)
    eye_dk = (er == ec).astype(bf16)
    n_per = 2 * DN_PAIRS
    rows_of = [slice(cc * C, (cc + 1) * C) for cc in range(DN_INTRA_CHUNKS)]
    gbs = [gb_ref[0, rows_of[cc], :] for cc in range(DN_INTRA_CHUNKS)]

    for cc in range(DN_INTRA_CHUNKS):
        rows = rows_of[cc]
        gb = gbs[cc]
        for pr in range(DN_PAIRS):
            h0, h1 = 2 * pr, 2 * pr + 1
            s0 = slice(h0 * DN_DK, (h0 + 1) * DN_DK)
            s1 = slice(h1 * DN_DK, (h1 + 1) * DN_DK)
            k0 = k_ref[0, rows, s0].astype(bf16)
            k1 = k_ref[0, rows, s1].astype(bf16)
            kbd = _block_diag2(k0, k1)
            kk_p = lax.dot_general(jnp.concatenate([k0, k1], axis=1), kbd, (((1,), (1,)), ((), ())),
                                   preferred_element_type=f32)
            qcat = jnp.concatenate([q_ref[0, rows, s0], q_ref[0, rows, s1]], axis=1).astype(bf16)
            qk_p = lax.dot_general(qcat, kbd, (((1,), (1,)), ((), ())), preferred_element_type=f32)
            kt_p = lax.dot_general(eye_dk, jnp.concatenate([k0, k1], axis=0), (((1,), (1,)), ((), ())),
                                   preferred_element_type=f32)
            for d in range(2):
                bl0, bl1 = d * DN_HEADS + h0, d * DN_HEADS + h1
                gl0, gl1 = 2 * DN_HEADS + bl0, 2 * DN_HEADS + bl1
                beta_p = jnp.take_along_axis(gb, jnp.where(left, bl0, bl1), axis=1)
                gc_p = jnp.take_along_axis(gb, jnp.where(left, gl0, gl1), axis=1)
                gc_row = jnp.concatenate([gbt_ref[0, cc, gl0:gl0 + 1, :], gbt_ref[0, cc, gl1:gl1 + 1, :]], axis=1)
                decay = jnp.exp(jnp.where(incl[d], gc_p - gc_row, -jnp.inf))
                n_p = jnp.where(strict[d], (beta_p * kk_p) * decay, 0.0)
                p = cc * n_per + pr * 2 + d
                np_scr[p] = n_p
                n_scr[p] = jnp.where(lo_h, n_p[:H], n_p[H:])
                qkkd_ref[d, 0, cc, pr, 0:C, :] = jnp.where(incl[d], qk_p * decay, 0.0).astype(bf16)
                g_end = gc_p[last[d]:last[d] + 1, :]
                qkkd_ref[d, 0, cc, pr, C:C + DN_DK, :] = (kt_p * jnp.exp(g_end - gc_row)).astype(bf16)

    t_packs = _diag_block_inverses(n_scr, [False, True] * (DN_PAIRS * DN_INTRA_CHUNKS))

    def scaled(cc, hh, d):
        hs = slice(hh * DN_DK, (hh + 1) * DN_DK)
        rows = rows_of[cc]
        bl = d * DN_HEADS + hh
        full = jnp.zeros((C, LANES), jnp.int32)
        beta = jnp.take_along_axis(gbs[cc], full + bl, axis=1)
        egc = jnp.exp(jnp.take_along_axis(gbs[cc], full + (2 * DN_HEADS + bl), axis=1))
        wq_ref[d, 0, cc, C:2 * C, hs] = (q_ref[0, rows, hs] * egc).astype(bf16)
        return jnp.concatenate([v_ref[0, rows, hs] * beta, k_ref[0, rows, hs] * (beta * egc)], axis=1)

    packs = [(cc, pr, d) for cc in range(DN_INTRA_CHUNKS) for pr in range(DN_PAIRS) for d in range(2)]
    hi_h = jnp.logical_not(lo_h)
    right_h = jnp.logical_not(left_h)
    xs = []
    for cc, pr, d in packs:
        p = cc * n_per + pr * 2 + d
        tp = t_packs[p]
        if d == 0:
            nb = np_scr[p, H:C, :]
            r1 = jnp.concatenate([zero_h, jnp.where(jnp.logical_and(lo_h, left_h), nb, 0.0), zero_h,
                                  jnp.where(jnp.logical_and(lo_h, right_h), nb, 0.0)], axis=0)
        else:
            nt = np_scr[p, 0:H, :]
            r1 = jnp.concatenate([jnp.where(jnp.logical_and(hi_h, left_h), nt, 0.0), zero_h,
                                  jnp.where(jnp.logical_and(hi_h, right_h), nt, 0.0), zero_h], axis=0)
        xs.append(_split3_dot(tp, r1))
    ys = []
    for (cc, pr, d), x in zip(packs, xs):
        tp = t_packs[cc * n_per + pr * 2 + d]
        if d == 0:
            r2 = jnp.concatenate([jnp.where(cih < H, tp, 0.0), zero_h,
                                  jnp.where(jnp.logical_and(cih >= C, cih < C + H), tp, 0.0), zero_h], axis=0)
        else:
            r2 = jnp.concatenate([zero_h, jnp.where(jnp.logical_and(cih >= H, cih < C), tp, 0.0),
                                  zero_h, jnp.where(cih >= C + H, tp, 0.0)], axis=0)
        ys.append(_split3_dot(x, r2))
    for (cc, pr, d), y in zip(packs, ys):
        h0, h1 = 2 * pr, 2 * pr + 1
        s0 = slice(h0 * DN_DK, (h0 + 1) * DN_DK)
        s1 = slice(h1 * DN_DK, (h1 + 1) * DN_DK)
        rows = rows_of[cc]
        tp = t_packs[cc * n_per + pr * 2 + d]
        if d == 0:
            t_top = jnp.where(lo_h, tp, 0.0)
            t_bot = jnp.where(lo_h, -y, tp)
        else:
            t_top = jnp.where(lo_h, tp, -y)
            t_bot = jnp.where(lo_h, 0.0, tp)
        t_p = jnp.concatenate([t_top, t_bot], axis=0)
        sol = _split3_dot(t_p, _block_diag2(scaled(cc, h0, d), scaled(cc, h1, d)))
        uc_ref[d, 0, rows, s0] = sol[:, 0:DN_DV]
        wq_ref[d, 0, cc, 0:C, s0] = sol[:, DN_DV:2 * DN_DV].astype(bf16)
        uc_ref[d, 0, rows, s1] = sol[:, 2 * DN_DV:3 * DN_DV]
        wq_ref[d, 0, cc, 0:C, s1] = sol[:, 3 * DN_DV:4 * DN_DV].astype(bf16)


def _dn_intra(q, k, v, gb, gbt):
    B, Lt, W = q.shape
    C = DN_CHUNK
    nc = Lt // C
    ch = DN_INTRA_CHUNKS
    tok = lambda w: pl.BlockSpec((1, ch * C, w), lambda b, i: (b, i, 0))
    return pl.pallas_call(
        _dn_intra_body,
        out_shape=(jax.ShapeDtypeStruct((2, B, Lt, W), f32),
                   jax.ShapeDtypeStruct((2, B, nc, 2 * C, W), bf16),
                   jax.ShapeDtypeStruct((2, B, nc, DN_PAIRS, C + DN_DK, 2 * C), bf16)),
        grid=(B, nc // ch),
        in_specs=[tok(W), tok(W), tok(W), tok(LANES),
                  pl.BlockSpec((1, ch, 16, C), lambda b, i: (b, i, 0, 0))],
        out_specs=(pl.BlockSpec((2, 1, ch * C, W), lambda b, i: (0, b, i, 0)),
                   pl.BlockSpec((2, 1, ch, 2 * C, W), lambda b, i: (0, b, i, 0, 0)),
                   pl.BlockSpec((2, 1, ch, DN_PAIRS, C + DN_DK, 2 * C), lambda b, i: (0, b, i, 0, 0, 0))),
        scratch_shapes=[pltpu.VMEM((ch * 2 * DN_PAIRS, C, 2 * C), f32),
                        pltpu.VMEM((ch * 2 * DN_PAIRS, DN_HALF, 2 * C), f32)],
        compiler_params=_cparams(("arbitrary", "arbitrary")),
        name="dn_intra",
    )(q, k, v, gb, gbt)


def _dn_state_body(*refs, rev):
    if rev:
        uc_ref, wq_ref, qkkd_ref, gb_ref, of_ref, o_ref, s_scr = refs
    else:
        uc_ref, wq_ref, qkkd_ref, gb_ref, o_ref, s_scr = refs
    C = DN_CHUNK
    d = 1 if rev else 0
    last = 0 if rev else C - 1

    @pl.when(pl.program_id(1) == 0)
    def _():
        s_scr[...] = jnp.zeros_like(s_scr)

    for bb in range(DN_STATE_BATCH):
        for pr in range(DN_PAIRS):
            heads = (2 * pr, 2 * pr + 1)
            s_old, o_part, vbs = [], [], []
            for hh in heads:
                hs = slice(hh * DN_DK, (hh + 1) * DN_DK)
                s = s_scr[bb, hh]
                r = jnp.dot(wq_ref[0, bb, 0, :, hs], s.astype(bf16), preferred_element_type=f32)
                vbs.append((uc_ref[0, bb, :, hs] - r[0:C]).astype(bf16))
                o_part.append(r[C:2 * C])
                s_old.append(s)
            r2 = jnp.dot(qkkd_ref[0, bb, 0, pr], _block_diag2(vbs[0], vbs[1]), preferred_element_type=f32)
            for n, hh in enumerate(heads):
                hs = slice(hh * DN_DK, (hh + 1) * DN_DK)
                vs = slice(n * DN_DV, (n + 1) * DN_DV)
                gl = 2 * DN_HEADS + d * DN_HEADS + hh
                eg = jnp.exp(gb_ref[bb, last:last + 1, gl:gl + 1])
                s_scr[bb, hh] = s_old[n] * eg + r2[C:C + DN_DK, vs]
                o_h = o_part[n] + r2[0:C, vs]
                if rev:
                    o_h = o_h + of_ref[bb, :, hs]
                o_ref[bb, :, hs] = o_h


def _dn_state(uc, wq, qkkd, gb, n_ctx_chunks, rev, o_fwd=None):
    _, B, Lt, W = uc.shape
    C = DN_CHUNK
    nc = Lt // C
    nb = DN_STATE_BATCH
    d = 1 if rev else 0
    if rev:
        cidx = lambda i: jnp.where(i < n_ctx_chunks, n_ctx_chunks - 1 - i, nc - 1 - (i - n_ctx_chunks))
    else:
        cidx = lambda i: i
    tok = lambda w: pl.BlockSpec((nb, C, w), lambda b, i: (b, cidx(i), 0))
    in_specs = [pl.BlockSpec((1, nb, C, W), lambda b, i: (d, b, cidx(i), 0)),
                pl.BlockSpec((1, nb, 1, 2 * C, W), lambda b, i: (d, b, cidx(i), 0, 0)),
                pl.BlockSpec((1, nb, 1, DN_PAIRS, C + DN_DK, 2 * C), lambda b, i: (d, b, cidx(i), 0, 0, 0)),
                tok(LANES)]
    args = [uc, wq, qkkd, gb]
    if rev:
        in_specs.append(tok(W))
        args.append(o_fwd)
    return pl.pallas_call(
        functools.partial(_dn_state_body, rev=rev),
        out_shape=jax.ShapeDtypeStruct((B, Lt, W), f32),
        grid=(B // nb, nc),
        in_specs=in_specs,
        out_specs=tok(W),
        scratch_shapes=[pltpu.VMEM((nb, DN_HEADS, DN_DK, DN_DV), f32)],
        compiler_params=_cparams(("arbitrary", "arbitrary")),
        name="dn_state_bwd" if rev else "dn_state_fwd",
    )(*args)


def _attn_body(q_ref, kp_ref, kc_ref, kn_ref, kx_ref, vp_ref, vc_ref, vn_ref, vx_ref, sink_ref, o_ref,
               *, n_ctx_blocks, seq_len):
    Tq = AT_BLOCK
    G = AT_HEADS // AT_KV
    i = pl.program_id(1)
    is_ctx = i < n_ctx_blocks
    nloc = 3 * Tq
    qpos = (i - n_ctx_blocks) * Tq + lax.broadcasted_iota(jnp.int32, (Tq, nloc), 0)
    kpos = (i - n_ctx_blocks - 1) * Tq + lax.broadcasted_iota(jnp.int32, (Tq, nloc), 1)
    ok = jnp.logical_and(jnp.abs(qpos - kpos) <= WINDOW, jnp.logical_and(kpos >= 0, kpos < seq_len))
    ok = jnp.logical_and(ok, jnp.logical_not(is_ctx))
    k_all = jnp.concatenate([kp_ref[0], kc_ref[0], kn_ref[0], kx_ref[0]], axis=0).astype(bf16)
    v_all = jnp.concatenate([vp_ref[0], vc_ref[0], vn_ref[0], vx_ref[0]], axis=0)
    lane_q = lax.broadcasted_iota(jnp.int32, (Tq, LANES), 1)
    lane_v = lax.broadcasted_iota(jnp.int32, v_all.shape, 1)
    ug = AT_UNIT_GROUPS
    units = [(hk, gp) for hk in range(AT_KV) for gp in range(G // ug)]
    nt = (((1,), (1,)), ((), ()))
    ok_u = jnp.concatenate([ok] * ug, axis=0)
    lane_u = jnp.concatenate([lane_q] * ug, axis=0)
    rhs = []
    for hk in range(AT_KV):
        mine = lane_v // AT_HD == hk
        rhs.append(jnp.concatenate([jnp.where(mine, v_all, 0.0), mine.astype(f32)], axis=1).astype(bf16))
    scores, sinks = [], []
    for hk, gp in units:
        gs = range(gp * ug, (gp + 1) * ug)
        qs = jnp.concatenate([jnp.where(lane_q // AT_HD == hk, q_ref[0, :, g * LANES:(g + 1) * LANES], 0.0)
                              for g in gs], axis=0).astype(bf16)
        s = lax.dot_general(qs, k_all, nt, preferred_element_type=f32)
        scores.append(jnp.concatenate([jnp.where(ok_u, s[:, :nloc], -jnp.inf), s[:, nloc:]], axis=1))
        sinks.append(jnp.concatenate([jnp.broadcast_to(sink_ref[0:1, hk * G + g:hk * G + g + 1], (Tq, 1))
                                      for g in gs], axis=0))
    maxes = [jnp.maximum(jnp.max(s, axis=-1, keepdims=True), sk) for s, sk in zip(scores, sinks)]
    probs = [jnp.exp(s - m).astype(bf16) for s, m in zip(scores, maxes)]
    outs = [jnp.dot(p, rhs[hk], preferred_element_type=f32) for p, (hk, gp) in zip(probs, units)]
    for gp in range(G // ug):
        num = jnp.zeros((ug * Tq, LANES), f32)
        den = jnp.zeros((ug * Tq, LANES), f32)
        for u, (hk, gg) in enumerate(units):
            if gg == gp:
                num = num + outs[u][:, 0:LANES]
                den = den + outs[u][:, LANES:2 * LANES] + jnp.where(lane_u // AT_HD == hk,
                                                                    jnp.exp(sinks[u] - maxes[u]), 0.0)
        o = num / den
        for n in range(ug):
            g = gp * ug + n
            o_ref[0, :, g * LANES:(g + 1) * LANES] = o[n * Tq:(n + 1) * Tq, :]


def _attention(aq, ak, av, sink_vec, n_ctx, seq_len):
    B, Lt, _ = aq.shape
    nb = Lt // AT_BLOCK
    n_ctx_blocks = n_ctx // AT_BLOCK
    blk = lambda w, f: pl.BlockSpec((1, AT_BLOCK, w), lambda b, i: (b, f(i), 0))
    lo, hi = n_ctx_blocks, nb - 1
    prev = lambda i: jnp.clip(i - 1, lo, hi)
    cur = lambda i: i
    nxt = lambda i: jnp.clip(i + 1, lo, hi)
    ctx_spec = pl.BlockSpec((1, n_ctx, AT_KW), lambda b, i: (b, 0, 0))
    body = functools.partial(_attn_body, n_ctx_blocks=n_ctx_blocks, seq_len=seq_len)
    return pl.pallas_call(
        body,
        out_shape=jax.ShapeDtypeStruct((B, Lt, AT_QW), f32),
        grid=(B, nb),
        in_specs=[blk(AT_QW, cur),
                  blk(AT_KW, prev), blk(AT_KW, cur), blk(AT_KW, nxt), ctx_spec,
                  blk(AT_KW, prev), blk(AT_KW, cur), blk(AT_KW, nxt), ctx_spec,
                  _const_spec((1, LANES))],
        out_specs=blk(AT_QW, cur),
        compiler_params=_cparams(("arbitrary", "arbitrary")),
        name="window_attn",
    )(aq, ak, ak, ak, ak, av, av, av, av, sink_vec)


def _merge_body(x_ref, mod_ref, g_ref, ys_ref, od_ref, z_ref, yc_ref,
                wg_ref, wglu_ref, ng_ref, wa_ref, wb_ref, wc_ref, wo_ref, o_ref):
    x = x_ref[0]
    h = _modnorm(x, g_ref[...], mod_ref[0, 1:2, :], mod_ref[0, 0:1, :]).astype(bf16)
    zs = jax.nn.gelu(ys_ref[...])
    ya = zs * jax.nn.sigmoid(_dot(zs, wglu_ref[...]))
    ng = ng_ref[...]
    parts = []
    for hh in range(DN_HEADS):
        hs = slice(hh * DN_DV, (hh + 1) * DN_DV)
        o = od_ref[0, :, hs]
        on = o * lax.rsqrt(jnp.mean(o * o, axis=-1, keepdims=True) + EPS) * ng
        parts.append(on * _silu(z_ref[0, :, hs]))
    yb = jnp.concatenate(parts, axis=1)
    yc = yc_ref[0]
    D = D_MODEL

    def gate(j):
        return jax.nn.sigmoid(jnp.dot(h, wg_ref[:, j * D:(j + 1) * D], preferred_element_type=f32))

    m = gate(0) * _dot(ya, wa_ref[...])
    m = m + gate(1) * _dot(yb, wb_ref[...])
    m = m + gate(2) * _dot(yc, wc_ref[...])
    mix = _dot(m, wo_ref[...])
    o_ref[0] = x + mod_ref[0, 2:3, :] * mix


def _merge(xcat, mods, g1, ys_tm, o_dn, dz, yc, w_gate, w_glu, dn_g, w_a, w_b, w_c, w_o, n_ctx_tiles):
    B, Lt, D = xcat.shape
    nt = Lt // TOK_TILE
    tok = lambda w: pl.BlockSpec((1, TOK_TILE, w), lambda b, t: (b, t, 0))
    return pl.pallas_call(
        _merge_body,
        out_shape=jax.ShapeDtypeStruct((B, Lt, D), f32),
        grid=(B, nt),
        in_specs=[tok(D),
                  pl.BlockSpec((1, N_MOD, D), _mod_index(n_ctx_tiles, B)),
                  _const_spec((1, D)),
                  pl.BlockSpec((TOK_TILE, SSM_WIDTH), lambda b, t: (t, b)),
                  tok(DN_W), tok(DN_W), tok(AT_QW),
                  _const_spec(w_gate.shape), _const_spec(w_glu.shape), _const_spec((1, DN_DV)),
                  _const_spec(w_a.shape), _const_spec(w_b.shape), _const_spec(w_c.shape),
                  _const_spec(w_o.shape)],
        out_specs=tok(D),
        compiler_params=_cparams(("arbitrary", "arbitrary")),
        name="merge_out",
    )(xcat, mods, g1, ys_tm, o_dn, dz, yc, w_gate, w_glu, dn_g, w_a, w_b, w_c, w_o)


FF_CHUNK = 1024


def _ffn_body(x_ref, mod_ref, g_ref, w1_ref, w2_ref, fg_ref, o_ref):
    x = x_ref[0]
    h = _modnorm(x, g_ref[...], mod_ref[0, 4:5, :], mod_ref[0, 3:4, :]).astype(bf16)
    acc = jnp.zeros(x.shape, f32)
    for j in range(D_FF // FF_CHUNK):
        a = jnp.dot(h, w1_ref[:, j * FF_CHUNK:(j + 1) * FF_CHUNK], preferred_element_type=f32)
        a = jnp.square(jnp.maximum(a, 0.0)).astype(bf16)
        acc = acc + jnp.dot(a, w2_ref[j * FF_CHUNK:(j + 1) * FF_CHUNK, :], preferred_element_type=f32)
    y = x + mod_ref[0, 5:6, :] * acc
    if fg_ref is not None:
        y = y * lax.rsqrt(jnp.mean(y * y, axis=-1, keepdims=True) + EPS) * fg_ref[...]
    o_ref[0] = y


def _ffn_mid_body(x_ref, mod_ref, g_ref, w1_ref, w2_ref, o_ref):
    _ffn_body(x_ref, mod_ref, g_ref, w1_ref, w2_ref, None, o_ref)


def _ffn(xcat, mods, g2, w1, w2, n_ctx_tiles, final_g=None):
    B, Lt, D = xcat.shape
    nt = Lt // TOK_TILE
    tok = pl.BlockSpec((1, TOK_TILE, D), lambda b, t: (b, t, 0))
    weights = [_const_spec((1, D)), _const_spec(w1.shape), _const_spec(w2.shape)]
    if final_g is None:
        return pl.pallas_call(
            _ffn_mid_body,
            out_shape=jax.ShapeDtypeStruct((B, Lt, D), f32),
            grid=(B, nt),
            in_specs=[tok, pl.BlockSpec((1, N_MOD, D), _mod_index(n_ctx_tiles, B))] + weights,
            out_specs=tok,
            compiler_params=_cparams(("arbitrary", "arbitrary")),
            name="ffn",
        )(xcat, mods, g2, w1, w2)
    return pl.pallas_call(
        _ffn_body,
        out_shape=jax.ShapeDtypeStruct((B, Lt - n_ctx_tiles * TOK_TILE, D), f32),
        grid=(B, nt - n_ctx_tiles),
        in_specs=[pl.BlockSpec((1, TOK_TILE, D), lambda b, t: (b, t + n_ctx_tiles, 0)),
                  pl.BlockSpec((1, N_MOD, D), lambda b, t: (b, 0, 0))] + weights + [_const_spec((1, D))],
        out_specs=tok,
        compiler_params=_cparams(("arbitrary", "arbitrary")),
        name="ffn_final",
    )(xcat, mods, g2, w1, w2, final_g)


def _rope_tables(n_ctx, seq_len):
    n = AT_HD // 4
    inv_freq = ROPE_BASE ** (-jnp.arange(n, dtype=f32) / n)
    pos = jnp.arange(seq_len, dtype=jnp.int32)
    rows = (pos // GRID_W).astype(f32)[:, None] * inv_freq[None, :]
    cols = (pos % GRID_W).astype(f32)[:, None] * inv_freq[None, :]
    cos = jnp.concatenate([jnp.cos(rows)] * 2 + [jnp.cos(cols)] * 2, axis=1)
    sin = jnp.concatenate([-jnp.sin(rows), jnp.sin(rows), -jnp.sin(cols), jnp.sin(cols)], axis=1)
    cos = jnp.concatenate([jnp.ones((n_ctx, AT_HD), f32), cos], axis=0)
    sin = jnp.concatenate([jnp.zeros((n_ctx, AT_HD), f32), sin], axis=0)
    return jnp.tile(cos, (1, LANES // AT_HD)), jnp.tile(sin, (1, LANES // AT_HD))


def _attn_head_perm():
    G = AT_HEADS // AT_KV
    n = jnp.arange(AT_QW)
    g, hk, e = n // LANES, (n % LANES) // AT_HD, n % AT_HD
    return (hk * G + g) * AT_HD + e


def _split_w_in(w_in):
    o = SSM_WIDTH + 4 * DN_W
    w_ba = w_in[:, o:o + 4 * DN_HEADS]
    o2 = o + 4 * DN_HEADS
    w_att = w_in[:, o2:o2 + AT_QW + 2 * AT_KW]
    w_att = jnp.concatenate([w_att[:, :AT_QW][:, _attn_head_perm()], w_att[:, AT_QW:]], axis=1)
    w_gate = w_in[:, o2 + AT_QW + 2 * AT_KW:]
    w_main = jnp.concatenate([w_in[:, :o], w_att], axis=1).astype(bf16)
    w_ba = jnp.pad(w_ba, ((0, 0), (0, LANES - 4 * DN_HEADS))).astype(bf16)
    return w_main, w_ba, w_gate.astype(bf16)


def kernel(x, c, ctx, c_ctx, norm1_g, norm2_g, w_mod, b_mod, w_in, ssm_lam_re, ssm_lam_im, ssm_log_dt, ssm_b_re, ssm_b_im, ssm_c_re, ssm_c_im, ssm_d, ssm_w_glu, dn_conv_w, dn_a_log, dn_dt_bias, dn_norm_g, attn_sink, w_branch_a, w_branch_b, w_branch_c, w_out, w_ff1, w_ff2, final_norm_g):
    B, L, D = x.shape
    Lc = ctx.shape[1]
    depth = w_in.shape[0]
    assert B == SUBLANES and D == D_MODEL
    assert Lc % TOK_TILE == 0 and L % TOK_TILE == 0
    Lt = Lc + L
    n_ctx_tiles = Lc // TOK_TILE

    cond = jnp.concatenate([c, c_ctx[None, :], jnp.zeros((16 - B - 1, D), f32)], axis=0)
    mods_all = _modulation(cond, w_mod, b_mod).reshape(depth, 16, N_MOD, D)
    cos_t, sin_t = _rope_tables(Lc, L)
    xcat = jnp.concatenate([ctx, x], axis=1)

    for layer in range(depth):
        mods = mods_all[layer]
        g1 = norm1_g[layer].reshape(1, D)
        g2 = norm2_g[layer].reshape(1, D)
        w_main, w_ba, w_gate = _split_w_in(w_in[layer])

        u_tm, dqkv, dz, dba, aq, ak, av = _inproj(xcat, mods, g1, w_main, w_ba, cos_t, sin_t, n_ctx_tiles)

        a_re, a_im, bbt_re, bbt_im = _s5_discretize(ssm_lam_re[layer], ssm_lam_im[layer], ssm_log_dt[layer],
                                                    ssm_b_re[layer], ssm_b_im[layer])
        bmat, cmat, avec = _s5_matrices(a_re, a_im, bbt_re, bbt_im, ssm_c_re[layer], ssm_c_im[layer])
        u_rows = u_tm.reshape(Lt * B, SSM_WIDTH)
        y_f = _s5_scan(u_rows, bmat, cmat, avec, Lc, rev=False)
        y_s5 = _s5_scan(u_rows, bmat, cmat, avec, Lc, rev=True, yf=y_f,
                        d_skip=ssm_d[layer].reshape(1, SSM_WIDTH))
        ys_tm = y_s5.reshape(Lt, B * SSM_WIDTH)

        conv_w = jnp.pad(dn_conv_w[layer], ((0, SUBLANES - DN_CONV), (0, 0)))
        pad_l = 2 * DN_HEADS
        al_vec = jnp.pad(dn_a_log[layer].reshape(1, 2 * DN_HEADS), ((0, 0), (pad_l, LANES - 2 * pad_l)))
        dtb_vec = jnp.pad(dn_dt_bias[layer].reshape(1, 2 * DN_HEADS), ((0, 0), (pad_l, LANES - 2 * pad_l)))
        qn, kn, vs, gb, gbt = _dn_prepare(dqkv, dba, conv_w, al_vec, dtb_vec, n_ctx_tiles)
        n_ctx_chunks = Lc // DN_CHUNK
        uc, wq, qkkd = _dn_intra(qn, kn, vs, gb, gbt)
        o_f = _dn_state(uc, wq, qkkd, gb, n_ctx_chunks, rev=False)
        o_dn = _dn_state(uc, wq, qkkd, gb, n_ctx_chunks, rev=True, o_fwd=o_f)

        sink_vec = jnp.pad(attn_sink[layer].reshape(1, AT_HEADS), ((0, 0), (0, LANES - AT_HEADS)))
        yc = _attention(aq, ak, av, sink_vec, Lc, L)

        x1 = _merge(xcat, mods, g1, ys_tm, o_dn, dz, yc, w_gate, ssm_w_glu[layer].astype(bf16),
                    dn_norm_g[layer].reshape(1, DN_DV), w_branch_a[layer].astype(bf16),
                    w_branch_b[layer].astype(bf16), w_branch_c[layer][_attn_head_perm(), :].astype(bf16),
                    w_out[layer].astype(bf16), n_ctx_tiles)
        final_g = final_norm_g.reshape(1, D) if layer == depth - 1 else None
        xcat = _ffn(x1, mods, g2, w_ff1[layer].astype(bf16), w_ff2[layer].astype(bf16), n_ctx_tiles, final_g)

    return xcat
```

```python
import functools
import math

import jax
import jax.numpy as jnp
from jax import lax
from jax.experimental import pallas as pl
from jax.experimental.pallas import tpu as pltpu

f32 = jnp.float32
bf16 = jnp.bfloat16
HI = lax.Precision.HIGHEST

D_MODEL = 1024
GRID_W = 64
EPS = 1e-6
SSM_WIDTH = D_MODEL // 2
SSM_GROUP = 16
SSM_GROUPS = SSM_WIDTH // SSM_GROUP
SSM_STATE = 64
DN_HEADS = 4
DN_DK = 128
DN_DV = 128
DN_CONV = 5
DN_CHUNK = 64
AT_HEADS = 8
AT_KV = 2
AT_HD = 64
WINDOW = 128
AT_BLOCK = 128
AT_UNIT_GROUPS = 1
ROPE_BASE = 10000.0
D_FF = 4 * D_MODEL
N_MOD = 6
DN_W = DN_HEADS * DN_DK
AT_QW = AT_HEADS * AT_HD
AT_KW = AT_KV * AT_HD

SUBLANES = 8
LANES = 128
TOK_TILE = 256
S5_STEPS = 128
S5_COLBLK = 128
S5_NBLK = SSM_WIDTH // S5_COLBLK
S5_SBLK = (S5_COLBLK // SSM_GROUP) * SSM_STATE
VMEM_LIMIT = 56 * 1024 * 1024


def _cparams(sem):
    return pltpu.CompilerParams(dimension_semantics=sem, vmem_limit_bytes=VMEM_LIMIT)


def _const_spec(shape):
    nd = len(shape)
    return pl.BlockSpec(shape, lambda *_: (0,) * nd, pipeline_mode=pl.Buffered(1))


def _dot(a, b):
    return jnp.dot(a.astype(bf16), b.astype(bf16), preferred_element_type=f32)


def _dot_nt(a, b):
    return lax.dot_general(a.astype(bf16), b.astype(bf16), (((1,), (1,)), ((), ())),
                           preferred_element_type=f32)


def _dot_tn(a, b):
    return lax.dot_general(a.astype(bf16), b.astype(bf16), (((0,), (0,)), ((), ())),
                           preferred_element_type=f32)


def _modnorm(x, g, scale, shift):
    y = x * lax.rsqrt(jnp.mean(x * x, axis=-1, keepdims=True) + EPS)
    return (y * g) * (1.0 + scale) + shift


def _silu(x):
    return x * jax.nn.sigmoid(x)


def _bf16_pieces(x):
    hi = x.astype(bf16)
    r = x - hi.astype(f32)
    mid = r.astype(bf16)
    lo = (r - mid.astype(f32)).astype(bf16)
    return [hi, mid, lo]


def _mod_body(c_ref, w_ref, b_ref, o_ref):
    s = _silu(c_ref[...])
    o_ref[0] = jnp.dot(s, w_ref[0], precision=HI, preferred_element_type=f32) + b_ref[0]


def _modulation(cond, w_mod, b_mod):
    depth = w_mod.shape[0]
    nblk = (N_MOD * D_MODEL) // D_MODEL
    return pl.pallas_call(
        _mod_body,
        out_shape=jax.ShapeDtypeStruct((depth, 16, N_MOD * D_MODEL), f32),
        grid=(depth, nblk),
        in_specs=[pl.BlockSpec((16, D_MODEL), lambda l, j: (0, 0)),
                  pl.BlockSpec((1, D_MODEL, D_MODEL), lambda l, j: (l, 0, j)),
                  pl.BlockSpec((1, 1, D_MODEL), lambda l, j: (l, 0, j))],
        out_specs=pl.BlockSpec((1, 16, D_MODEL), lambda l, j: (l, 0, j)),
        compiler_params=_cparams(("arbitrary", "arbitrary")),
        name="adaln_mod",
    )(cond, w_mod, b_mod.reshape(depth, 1, N_MOD * D_MODEL))


def _rope(x, cos, sin):
    n = x.shape[-1]
    lane = lax.broadcasted_iota(jnp.int32, x.shape, 1)
    partner = jnp.where((lane % 32) < 16, pltpu.roll(x, n - 16, 1), pltpu.roll(x, 16, 1))
    return x * cos + partner * sin


def _inproj_body(x_ref, mod_ref, g_ref, wm_ref, wba_ref, cos_ref, sin_ref, cw_ref, al_ref, dtb_ref,
                 u_ref, z_ref, aq_ref, ak_ref, av_ref, q_ref, k_ref, v_ref, gb_ref, gbt_ref,
                 qkv_scr, ba_scr, halo_scr, *, n_ctx_tiles, n_tiles):
    T = TOK_TILE
    t = pl.program_id(1)
    cur = jnp.bitwise_and(t, 1)
    prv = 1 - cur

    @pl.when(jnp.logical_and(pl.program_id(0) == 0, t == 0))
    def _():
        qkv_scr[...] = jnp.zeros_like(qkv_scr)
        ba_scr[...] = jnp.zeros_like(ba_scr)
        halo_scr[...] = jnp.zeros_like(halo_scr)

    h = _modnorm(x_ref[0], g_ref[...], mod_ref[0, 1:2, :], mod_ref[0, 0:1, :]).astype(bf16)

    def proj(lo, hi):
        return jnp.dot(h, wm_ref[:, lo:hi], preferred_element_type=f32)

    o_u = 0
    o_qkv = o_u + SSM_WIDTH
    o_z = o_qkv + 3 * DN_W
    o_aq = o_z + DN_W
    o_ak = o_aq + AT_QW
    o_av = o_ak + AT_KW
    qkv_scr[cur, SUBLANES:SUBLANES + T, :] = proj(o_qkv, o_z)

    tp = t - 1
    prev_ok = jnp.logical_and(tp != 0, tp != n_ctx_tiles)
    next_ok = jnp.logical_and(tp != n_ctx_tiles - 1, tp != n_tiles - 1)
    qkv_scr[prv, 0:SUBLANES, :] = jnp.where(prev_ok, halo_scr[...], 0.0)
    qkv_scr[prv, SUBLANES + T:2 * SUBLANES + T, :] = jnp.where(next_ok, qkv_scr[cur, SUBLANES:2 * SUBLANES, :], 0.0)
    halo_scr[...] = qkv_scr[prv, T:T + SUBLANES, :]
    half = DN_CONV // 2

    def conv_act(cs):
        acc = cw_ref[0:1, cs] * qkv_scr[prv, pl.ds(SUBLANES - half, T), cs]
        for kk in range(1, DN_CONV):
            acc = acc + cw_ref[kk:kk + 1, cs] * qkv_scr[prv, pl.ds(SUBLANES - half + kk, T), cs]
        return _silu(acc)

    def prep_q(hh):
        hs = slice(hh * DN_DK, (hh + 1) * DN_DK)
        qs = conv_act(hs)
        q_ref[0, :, hs] = qs * (lax.rsqrt(jnp.sum(qs * qs, axis=-1, keepdims=True) + EPS) * (DN_DK ** -0.5))

    def prep_k(hh):
        hs = slice(hh * DN_DK, (hh + 1) * DN_DK)
        ks = conv_act(slice(DN_W + hh * DN_DK, DN_W + (hh + 1) * DN_DK))
        k_ref[0, :, hs] = ks * lax.rsqrt(jnp.sum(ks * ks, axis=-1, keepdims=True) + EPS)

    def prep_v(hh):
        hs = slice(hh * DN_DV, (hh + 1) * DN_DV)
        v_ref[0, :, hs] = conv_act(slice(2 * DN_W + hh * DN_DV, 2 * DN_W + (hh + 1) * DN_DV))

    cos = cos_ref[...]
    sin = sin_ref[...]
    reps = AT_QW // LANES
    hw = SSM_WIDTH // 2

    def proj_u(n):
        u_ref[:, n * hw:(n + 1) * hw] = proj(o_u + n * hw, o_u + (n + 1) * hw)

    def proj_z(n):
        z_ref[0, :, n * hw:(n + 1) * hw] = proj(o_z + n * hw, o_z + (n + 1) * hw)

    def proj_aq():
        aq_ref[0] = _rope(proj(o_aq, o_ak), jnp.concatenate([cos] * reps, axis=1),
                          jnp.concatenate([sin] * reps, axis=1)) * (AT_HD ** -0.5)

    def proj_akv():
        ak_ref[0] = _rope(proj(o_ak, o_av), cos, sin)
        av_ref[0] = proj(o_av, o_av + AT_KW)
        ba_scr[cur] = jnp.dot(h, wba_ref[...], preferred_element_type=f32)

    mxu_work = [lambda: proj_u(0), lambda: proj_u(1), lambda: proj_z(0), lambda: proj_z(1), proj_aq, proj_akv]
    vpu_work = ([functools.partial(prep_q, hh) for hh in range(DN_HEADS)]
                + [functools.partial(prep_k, hh) for hh in range(DN_HEADS)]
                + [functools.partial(prep_v, hh) for hh in range(DN_HEADS)])
    per = len(vpu_work) // len(mxu_work)
    for n, mx in enumerate(mxu_work):
        mx()
        for vp in vpu_work[n * per:(n + 1) * per]:
            vp()

    ba = ba_scr[prv]
    lane = lax.broadcasted_iota(jnp.int32, ba.shape, 1)
    beta = jax.nn.sigmoid(ba)
    g = -jnp.exp(al_ref[...]) * jax.nn.softplus(ba + dtb_ref[...])
    ri = lax.broadcasted_iota(jnp.int32, (T, T), 0)
    ci = lax.broadcasted_iota(jnp.int32, (T, T), 1)
    same = (ri // DN_CHUNK) == (ci // DN_CHUNK)
    tri_f = jnp.logical_and(same, ri >= ci).astype(bf16)
    tri_b = jnp.logical_and(same, ri <= ci).astype(bf16)
    g3 = jnp.concatenate(_bf16_pieces(g), axis=0)
    gc_f = jnp.dot(jnp.concatenate([tri_f] * 3, axis=1), g3, preferred_element_type=f32)
    gc_b = jnp.dot(jnp.concatenate([tri_b] * 3, axis=1), g3, preferred_element_type=f32)
    nh = DN_HEADS
    gc = jnp.where(lane < 2 * nh + nh, gc_f, gc_b)
    gb = jnp.where(lane < 2 * nh, beta, jnp.where(lane < 4 * nh, gc, 0.0))
    gb_ref[0] = gb
    er = lax.broadcasted_iota(jnp.int32, (16, 3 * LANES), 0)
    ec = lax.broadcasted_iota(jnp.int32, (16, 3 * LANES), 1)
    sel = (er == ec % LANES).astype(bf16)
    gbt = lax.dot_general(sel, jnp.concatenate(_bf16_pieces(gb), axis=1), (((1,), (1,)), ((), ())),
                          preferred_element_type=f32)
    for c in range(T // DN_CHUNK):
        gbt_ref[0, c] = gbt[:, c * DN_CHUNK:(c + 1) * DN_CHUNK]


def _mod_index(n_ctx_tiles, ctx_row):
    return lambda b, t: (jnp.where(t < n_ctx_tiles, ctx_row, b), 0, 0)


def _inproj(xcat, mods, g1, w_main, w_ba, cos_t, sin_t, conv_w, al_vec, dtb_vec, n_ctx_tiles):
    B, Lt, D = xcat.shape
    nt = Lt // TOK_TILE
    wm = w_main.shape[1]
    tc = lambda t: jnp.minimum(t, nt - 1)
    tp = lambda t: jnp.maximum(t - 1, 0)
    cur = lambda w: pl.BlockSpec((1, TOK_TILE, w), lambda b, t: (b, tc(t), 0))
    prev = lambda w: pl.BlockSpec((1, TOK_TILE, w), lambda b, t: (b, tp(t), 0))
    mod_idx = _mod_index(n_ctx_tiles, B)
    out_shape = (
        jax.ShapeDtypeStruct((Lt, B * SSM_WIDTH), f32),
        jax.ShapeDtypeStruct((B, Lt, DN_W), f32),
        jax.ShapeDtypeStruct((B, Lt, AT_QW), f32),
        jax.ShapeDtypeStruct((B, Lt, AT_KW), f32),
        jax.ShapeDtypeStruct((B, Lt, AT_KW), f32),
        jax.ShapeDtypeStruct((B, Lt, DN_W), f32),
        jax.ShapeDtypeStruct((B, Lt, DN_W), f32),
        jax.ShapeDtypeStruct((B, Lt, DN_W), f32),
        jax.ShapeDtypeStruct((B, Lt, LANES), f32),
        jax.ShapeDtypeStruct((B, Lt // DN_CHUNK, 16, DN_CHUNK), f32),
    )
    body = functools.partial(_inproj_body, n_ctx_tiles=n_ctx_tiles, n_tiles=nt)
    return pl.pallas_call(
        body,
        out_shape=out_shape,
        grid=(B, nt + 1),
        in_specs=[cur(D),
                  pl.BlockSpec((1, N_MOD, D), lambda b, t: mod_idx(b, tc(t))),
                  _const_spec((1, D)),
                  _const_spec((D, wm)),
                  _const_spec((D, LANES)),
                  pl.BlockSpec((TOK_TILE, LANES), lambda b, t: (tc(t), 0)),
                  pl.BlockSpec((TOK_TILE, LANES), lambda b, t: (tc(t), 0)),
                  _const_spec((SUBLANES, 3 * DN_W)),
                  _const_spec((1, LANES)),
                  _const_spec((1, LANES))],
        out_specs=(pl.BlockSpec((TOK_TILE, SSM_WIDTH), lambda b, t: (tc(t), b)),
                   cur(DN_W), cur(AT_QW), cur(AT_KW), cur(AT_KW),
                   prev(DN_W), prev(DN_W), prev(DN_W), prev(LANES),
                   pl.BlockSpec((1, TOK_TILE // DN_CHUNK, 16, DN_CHUNK), lambda b, t: (b, tp(t), 0, 0))),
        scratch_shapes=[pltpu.VMEM((2, TOK_TILE + 2 * SUBLANES, 3 * DN_W), f32),
                        pltpu.VMEM((2, TOK_TILE, LANES), f32),
                        pltpu.VMEM((SUBLANES, 3 * DN_W), f32)],
        compiler_params=_cparams(("arbitrary", "arbitrary")),
        name="in_proj",
    )(xcat, mods, g1, w_main, w_ba, cos_t, sin_t, conv_w, al_vec, dtb_vec)


def _s5disc_body(lre_ref, lim_ref, ldt_ref, bre_ref, bim_ref, are_ref, aim_ref, bbre_ref, bbim_ref):
    lr = lre_ref[0]
    li = lim_ref[0]
    dt = jnp.exp(ldt_ref[0])
    mag = jnp.exp(lr * dt)
    a_re = mag * jnp.cos(li * dt)
    a_im = mag * jnp.sin(li * dt)
    den = lr * lr + li * li
    f_re = ((a_re - 1.0) * lr + a_im * li) / den
    f_im = (a_im * lr - (a_re - 1.0) * li) / den
    are_ref[0] = a_re
    aim_ref[0] = a_im
    b_re = bre_ref[0]
    b_im = bim_ref[0]
    bbre_ref[0] = f_re * b_re - f_im * b_im
    bbim_ref[0] = f_re * b_im + f_im * b_re


def _s5_discretize(lam_re, lam_im, log_dt, b_re, b_im):
    n = 2 * SSM_GROUPS
    P, H = SSM_STATE, SSM_GROUP
    vec = pl.BlockSpec((1, 1, P), lambda i: (i, 0, 0))
    mat = pl.BlockSpec((1, H, P), lambda i: (i, 0, 0))
    return pl.pallas_call(
        _s5disc_body,
        out_shape=(jax.ShapeDtypeStruct((n, 1, P), f32), jax.ShapeDtypeStruct((n, 1, P), f32),
                   jax.ShapeDtypeStruct((n, H, P), f32), jax.ShapeDtypeStruct((n, H, P), f32)),
        grid=(n,),
        in_specs=[vec, vec, pl.BlockSpec((1, 1, 1), lambda i: (i, 0, 0)), mat, mat],
        out_specs=(vec, vec, mat, mat),
        compiler_params=_cparams(("arbitrary",)),
        name="s5_discretize",
    )(lam_re.reshape(n, 1, P), lam_im.reshape(n, 1, P), log_dt.reshape(n, 1, 1),
      jnp.swapaxes(b_re, -1, -2).reshape(n, H, P), jnp.swapaxes(b_im, -1, -2).reshape(n, H, P))


def _s5_matrices(a_re, a_im, bbt_re, bbt_im, c_re, c_im):
    gb = S5_COLBLK // SSM_GROUP
    P, H = SSM_STATE, SSM_GROUP
    eye = jnp.eye(gb, dtype=f32)

    def bblk(t):
        t = t.reshape(2, S5_NBLK, gb, H, P)
        return jnp.einsum('dcghp,gk->dcghkp', t, eye).reshape(2, S5_NBLK, gb * H, gb * P)

    def cblk(t):
        t = t.reshape(2, S5_NBLK, gb, H, P)
        return jnp.einsum('dcghp,gk->dcgpkh', t, eye).reshape(2, S5_NBLK, gb * P, gb * H)

    bmat = jnp.concatenate([bblk(bbt_re), bblk(bbt_im)], axis=-1).astype(bf16)
    cmat = jnp.concatenate([cblk(c_re.astype(f32)), -cblk(c_im.astype(f32))], axis=-2).astype(bf16)
    avec = jnp.stack([a_re.reshape(2, S5_NBLK, gb * P), a_im.reshape(2, S5_NBLK, gb * P)], axis=2)
    return bmat, cmat, avec


def _s5_scan_body(*refs, rev):
    if rev:
        u_ref, bm_ref, cm_ref, a_ref, yf_ref, d_ref, y_ref, bu_scr, st_scr = refs
    else:
        u_ref, bm_ref, cm_ref, a_ref, y_ref, bu_scr, st_scr = refs
    B = SUBLANES
    nsb = S5_SBLK

    @pl.when(pl.program_id(0) == 0)
    def _():
        st_scr[...] = jnp.zeros_like(st_scr)

    def tile_rows(t):
        return pl.ds(((S5_STEPS - 1 - t) if rev else t) * B, B)

    for cb in range(S5_NBLK):
        cols = slice(cb * S5_COLBLK, (cb + 1) * S5_COLBLK)
        if rev:
            u_blk = jnp.concatenate([u_ref[tile_rows(t), cols] for t in range(S5_STEPS)], axis=0)
        else:
            u_blk = u_ref[:, cols]
        bu_scr[cb] = jnp.dot(u_blk.astype(bf16), bm_ref[0, cb], preferred_element_type=f32)
        are = jnp.broadcast_to(a_ref[0, cb, 0:1, :], (B, nsb))
        aim = jnp.broadcast_to(a_ref[0, cb, 1:2, :], (B, nsb))
        sre = st_scr[cb, :, 0:nsb]
        sim = st_scr[cb, :, nsb:2 * nsb]
        for t in range(S5_STEPS):
            rows = pl.ds(t * B, B)
            nre = are * sre - aim * sim + bu_scr[cb, rows, 0:nsb]
            nim = are * sim + aim * sre + bu_scr[cb, rows, nsb:2 * nsb]
            bu_scr[cb, rows, 0:nsb] = nre
            bu_scr[cb, rows, nsb:2 * nsb] = nim
            sre, sim = nre, nim
        st_scr[cb, :, 0:nsb] = sre
        st_scr[cb, :, nsb:2 * nsb] = sim
        y = jnp.dot(bu_scr[cb].astype(bf16), cm_ref[0, cb], preferred_element_type=f32)
        if rev:
            for t in range(S5_STEPS):
                rows = tile_rows(t)
                y_ref[rows, cols] = (y[t * B:(t + 1) * B, :] + yf_ref[rows, cols]
                                     + d_ref[:, cols] * u_ref[rows, cols])
        else:
            y_ref[:, cols] = y


def _s5_tile_index(n_ctx_tiles, n_tiles, rev):
    if not rev:
        return lambda i: (i, 0)
    return lambda i: (jnp.where(i < n_ctx_tiles, n_ctx_tiles - 1 - i, n_tiles - 1 - (i - n_ctx_tiles)), 0)


def _s5_scan(u_rows, bmat, cmat, avec, n_ctx_steps, rev, yf=None, d_skip=None):
    n_rows = u_rows.shape[0]
    rows = S5_STEPS * SUBLANES
    n_tiles = n_rows // rows
    n_ctx_tiles = n_ctx_steps // S5_STEPS
    idx = _s5_tile_index(n_ctx_tiles, n_tiles, rev)
    d = 1 if rev else 0
    tile = pl.BlockSpec((rows, SSM_WIDTH), idx)
    in_specs = [tile,
                pl.BlockSpec((1, S5_NBLK, S5_COLBLK, 2 * S5_SBLK), lambda i: (d, 0, 0, 0), pipeline_mode=pl.Buffered(1)),
                pl.BlockSpec((1, S5_NBLK, 2 * S5_SBLK, S5_COLBLK), lambda i: (d, 0, 0, 0), pipeline_mode=pl.Buffered(1)),
                pl.BlockSpec((1, S5_NBLK, 2, S5_SBLK), lambda i: (d, 0, 0, 0), pipeline_mode=pl.Buffered(1))]
    args = [u_rows, bmat, cmat, avec]
    if rev:
        in_specs += [tile, _const_spec((1, SSM_WIDTH))]
        args += [yf, d_skip]
    return pl.pallas_call(
        functools.partial(_s5_scan_body, rev=rev),
        out_shape=jax.ShapeDtypeStruct((n_rows, SSM_WIDTH), f32),
        grid=(n_tiles,),
        in_specs=in_specs,
        out_specs=tile,
        scratch_shapes=[pltpu.VMEM((S5_NBLK, rows, 2 * S5_SBLK), f32),
                        pltpu.VMEM((S5_NBLK, SUBLANES, 2 * S5_SBLK), f32)],
        compiler_params=_cparams(("arbitrary",)),
        name="s5_scan_bwd" if rev else "s5_scan_fwd",
    )(*args)


DN_HALF = DN_CHUNK // 2
DN_PAIRS = DN_HEADS // 2
DN_STATE_BATCH = 8
DN_INTRA_CHUNKS = 4


def _diag_block_inverses(n_scr, revs):
    H = DN_HALF
    ng = H // SUBLANES
    ri = lax.broadcasted_iota(jnp.int32, (SUBLANES, LANES), 0)
    ci = lax.broadcasted_iota(jnp.int32, (SUBLANES, LANES), 1)
    eye = [(ri + SUBLANES * g == ci % H).astype(f32) for g in range(ng)]
    base = (ci // H) * H
    ts = [list(eye) for _ in revs]
    for step in range(H - 1):
        for p, rev in enumerate(revs):
            j = H - 1 - step if rev else step
            gj = j // SUBLANES
            idx = base + j
            row = ts[p][gj][j % SUBLANES:j % SUBLANES + 1, :]
            for g in (range(0, gj + 1) if rev else range(gj, ng)):
                col = jnp.take_along_axis(n_scr[p, SUBLANES * g:SUBLANES * (g + 1), :], idx, axis=1)
                ts[p][g] = ts[p][g] - col * row
    return [jnp.concatenate(t, axis=0) for t in ts]


def _split3_dot(a, b):
    a_hi = a.astype(bf16)
    a_lo = (a - a_hi.astype(f32)).astype(bf16)
    b_hi = b.astype(bf16)
    b_lo = (b - b_hi.astype(f32)).astype(bf16)
    return jnp.dot(jnp.concatenate([a_hi, a_hi, a_lo], axis=1), jnp.concatenate([b_hi, b_lo, b_hi], axis=0),
                   preferred_element_type=f32)


def _block_diag2(a, b):
    z = jnp.zeros_like(a)
    return jnp.concatenate([jnp.concatenate([a, z], axis=1), jnp.concatenate([z, b], axis=1)], axis=0)


def _dn_intra_body(q_ref, k_ref, v_ref, gb_ref, gbt_ref, uc_ref, wq_ref, qkkd_ref, np_scr, n_scr):
    C = DN_CHUNK
    H = DN_HALF
    ri = lax.broadcasted_iota(jnp.int32, (C, 2 * C), 0)
    ci = lax.broadcasted_iota(jnp.int32, (C, 2 * C), 1)
    cl = ci % C
    incl = (ri >= cl, ri <= cl)
    strict = (ri > cl, ri < cl)
    left = ci < C
    cih = lax.broadcasted_iota(jnp.int32, (H, 2 * C), 1)
    lo_h = (cih % C) < H
    left_h = cih < C
    zero_h = jnp.zeros((H, 2 * C), f32)
    last = (C - 1, 0)
    er = lax.broadcasted_iota(jnp.int32, (DN_DK, DN_DK), 0)
    ec = lax.broadcasted_iota(jnp.int32, (DN_DK, DN_DK), 1)
    eye_dk = (er == ec).astype(bf16)
    n_per = 2 * DN_PAIRS
    rows_of = [slice(cc * C, (cc + 1) * C) for cc in range(DN_INTRA_CHUNKS)]
    gbs = [gb_ref[0, rows_of[cc], :] for cc in range(DN_INTRA_CHUNKS)]

    for cc in range(DN_INTRA_CHUNKS):
        rows = rows_of[cc]
        gb = gbs[cc]
        for pr in range(DN_PAIRS):
            h0, h1 = 2 * pr, 2 * pr + 1
            s0 = slice(h0 * DN_DK, (h0 + 1) * DN_DK)
            s1 = slice(h1 * DN_DK, (h1 + 1) * DN_DK)
            k0 = k_ref[0, rows, s0].astype(bf16)
            k1 = k_ref[0, rows, s1].astype(bf16)
            kbd = _block_diag2(k0, k1)
            kk_p = lax.dot_general(jnp.concatenate([k0, k1], axis=1), kbd, (((1,), (1,)), ((), ())),
                                   preferred_element_type=f32)
            qcat = jnp.concatenate([q_ref[0, rows, s0], q_ref[0, rows, s1]], axis=1).astype(bf16)
            qk_p = lax.dot_general(qcat, kbd, (((1,), (1,)), ((), ())), preferred_element_type=f32)
            kt_p = lax.dot_general(eye_dk, jnp.concatenate([k0, k1], axis=0), (((1,), (1,)), ((), ())),
                                   preferred_element_type=f32)
            for d in range(2):
                bl0, bl1 = d * DN_HEADS + h0, d * DN_HEADS + h1
                gl0, gl1 = 2 * DN_HEADS + bl0, 2 * DN_HEADS + bl1
                beta_p = jnp.take_along_axis(gb, jnp.where(left, bl0, bl1), axis=1)
                gc_p = jnp.take_along_axis(gb, jnp.where(left, gl0, gl1), axis=1)
                gc_row = jnp.concatenate([gbt_ref[0, cc, gl0:gl0 + 1, :], gbt_ref[0, cc, gl1:gl1 + 1, :]], axis=1)
                decay = jnp.exp(jnp.where(incl[d], gc_p - gc_row, -jnp.inf))
                n_p = jnp.where(strict[d], (beta_p * kk_p) * decay, 0.0)
                p = cc * n_per + pr * 2 + d
                np_scr[p] = n_p
                n_scr[p] = jnp.where(lo_h, n_p[:H], n_p[H:])
                qkkd_ref[d, 0, cc, pr, 0:C, :] = jnp.where(incl[d], qk_p * decay, 0.0).astype(bf16)
                g_end = gc_p[last[d]:last[d] + 1, :]
                qkkd_ref[d, 0, cc, pr, C:C + DN_DK, :] = (kt_p * jnp.exp(g_end - gc_row)).astype(bf16)

    t_packs = _diag_block_inverses(n_scr, [False, True] * (DN_PAIRS * DN_INTRA_CHUNKS))

    def scaled(cc, hh, d):
        hs = slice(hh * DN_DK, (hh + 1) * DN_DK)
        rows = rows_of[cc]
        bl = d * DN_HEADS + hh
        full = jnp.zeros((C, LANES), jnp.int32)
        beta = jnp.take_along_axis(gbs[cc], full + bl, axis=1)
        egc = jnp.exp(jnp.take_along_axis(gbs[cc], full + (2 * DN_HEADS + bl), axis=1))
        wq_ref[d, 0, cc, C:2 * C, hs] = (q_ref[0, rows, hs] * egc).astype(bf16)
        return jnp.concatenate([v_ref[0, rows, hs] * beta, k_ref[0, rows, hs] * (beta * egc)], axis=1)

    packs = [(cc, pr, d) for cc in range(DN_INTRA_CHUNKS) for pr in range(DN_PAIRS) for d in range(2)]
    hi_h = jnp.logical_not(lo_h)
    right_h = jnp.logical_not(left_h)
    xs = []
    for cc, pr, d in packs:
        p = cc * n_per + pr * 2 + d
        tp = t_packs[p]
        if d == 0:
            nb = np_scr[p, H:C, :]
            r1 = jnp.concatenate([zero_h, jnp.where(jnp.logical_and(lo_h, left_h), nb, 0.0), zero_h,
                                  jnp.where(jnp.logical_and(lo_h, right_h), nb, 0.0)], axis=0)
        else:
            nt = np_scr[p, 0:H, :]
            r1 = jnp.concatenate([jnp.where(jnp.logical_and(hi_h, left_h), nt, 0.0), zero_h,
                                  jnp.where(jnp.logical_and(hi_h, right_h), nt, 0.0), zero_h], axis=0)
        xs.append(_split3_dot(tp, r1))
    ys = []
    for (cc, pr, d), x in zip(packs, xs):
        tp = t_packs[cc * n_per + pr * 2 + d]
        if d == 0:
            r2 = jnp.concatenate([jnp.where(cih < H, tp, 0.0), zero_h,
                                  jnp.where(jnp.logical_and(cih >= C, cih < C + H), tp, 0.0), zero_h], axis=0)
        else:
            r2 = jnp.concatenate([zero_h, jnp.where(jnp.logical_and(cih >= H, cih < C), tp, 0.0),
                                  zero_h, jnp.where(cih >= C + H, tp, 0.0)], axis=0)
        ys.append(_split3_dot(x, r2))
    for (cc, pr, d), y in zip(packs, ys):
        h0, h1 = 2 * pr, 2 * pr + 1
        s0 = slice(h0 * DN_DK, (h0 + 1) * DN_DK)
        s1 = slice(h1 * DN_DK, (h1 + 1) * DN_DK)
        rows = rows_of[cc]
        tp = t_packs[cc * n_per + pr * 2 + d]
        if d == 0:
            t_top = jnp.where(lo_h, tp, 0.0)
            t_bot = jnp.where(lo_h, -y, tp)
        else:
            t_top = jnp.where(lo_h, tp, -y)
            t_bot = jnp.where(lo_h, 0.0, tp)
        t_p = jnp.concatenate([t_top, t_bot], axis=0)
        rhs_bd = _block_diag2(scaled(cc, h0, d), scaled(cc, h1, d)).astype(bf16)
        t_hi = t_p.astype(bf16)
        t_lo = (t_p - t_hi.astype(f32)).astype(bf16)
        sol = jnp.dot(jnp.concatenate([t_hi, t_lo], axis=1), jnp.concatenate([rhs_bd, rhs_bd], axis=0),
                      preferred_element_type=f32)
        uc_ref[d, 0, rows, s0] = sol[:, 0:DN_DV]
        wq_ref[d, 0, cc, 0:C, s0] = sol[:, DN_DV:2 * DN_DV].astype(bf16)
        uc_ref[d, 0, rows, s1] = sol[:, 2 * DN_DV:3 * DN_DV]
        wq_ref[d, 0, cc, 0:C, s1] = sol[:, 3 * DN_DV:4 * DN_DV].astype(bf16)


def _dn_intra(q, k, v, gb, gbt):
    B, Lt, W = q.shape
    C = DN_CHUNK
    nc = Lt // C
    ch = DN_INTRA_CHUNKS
    tok = lambda w: pl.BlockSpec((1, ch * C, w), lambda b, i: (b, i, 0))
    return pl.pallas_call(
        _dn_intra_body,
        out_shape=(jax.ShapeDtypeStruct((2, B, Lt, W), f32),
                   jax.ShapeDtypeStruct((2, B, nc, 2 * C, W), bf16),
                   jax.ShapeDtypeStruct((2, B, nc, DN_PAIRS, C + DN_DK, 2 * C), bf16)),
        grid=(B, nc // ch),
        in_specs=[tok(W), tok(W), tok(W), tok(LANES),
                  pl.BlockSpec((1, ch, 16, C), lambda b, i: (b, i, 0, 0))],
        out_specs=(pl.BlockSpec((2, 1, ch * C, W), lambda b, i: (0, b, i, 0)),
                   pl.BlockSpec((2, 1, ch, 2 * C, W), lambda b, i: (0, b, i, 0, 0)),
                   pl.BlockSpec((2, 1, ch, DN_PAIRS, C + DN_DK, 2 * C), lambda b, i: (0, b, i, 0, 0, 0))),
        scratch_shapes=[pltpu.VMEM((ch * 2 * DN_PAIRS, C, 2 * C), f32),
                        pltpu.VMEM((ch * 2 * DN_PAIRS, DN_HALF, 2 * C), f32)],
        compiler_params=_cparams(("arbitrary", "arbitrary")),
        name="dn_intra",
    )(q, k, v, gb, gbt)


def _dn_state_body(*refs, rev):
    if rev:
        uc_ref, wq_ref, qkkd_ref, gb_ref, of_ref, o_ref, s_scr = refs
    else:
        uc_ref, wq_ref, qkkd_ref, gb_ref, o_ref, s_scr = refs
    C = DN_CHUNK
    d = 1 if rev else 0
    last = 0 if rev else C - 1

    @pl.when(pl.program_id(1) == 0)
    def _():
        s_scr[...] = jnp.zeros_like(s_scr)

    for bb in range(DN_STATE_BATCH):
        for pr in range(DN_PAIRS):
            heads = (2 * pr, 2 * pr + 1)
            s_old, o_part, vbs = [], [], []
            for hh in heads:
                hs = slice(hh * DN_DK, (hh + 1) * DN_DK)
                s = s_scr[bb, hh]
                r = jnp.dot(wq_ref[0, bb, 0, :, hs], s.astype(bf16), preferred_element_type=f32)
                vbs.append((uc_ref[0, bb, :, hs] - r[0:C]).astype(bf16))
                o_part.append(r[C:2 * C])
                s_old.append(s)
            r2 = jnp.dot(qkkd_ref[0, bb, 0, pr], _block_diag2(vbs[0], vbs[1]), preferred_element_type=f32)
            for n, hh in enumerate(heads):
                hs = slice(hh * DN_DK, (hh + 1) * DN_DK)
                vs = slice(n * DN_DV, (n + 1) * DN_DV)
                gl = 2 * DN_HEADS + d * DN_HEADS + hh
                eg = jnp.exp(gb_ref[bb, last:last + 1, gl:gl + 1])
                s_scr[bb, hh] = s_old[n] * eg + r2[C:C + DN_DK, vs]
                o_h = o_part[n] + r2[0:C, vs]
                if rev:
                    o_h = o_h + of_ref[bb, :, hs]
                o_ref[bb, :, hs] = o_h


def _dn_state(uc, wq, qkkd, gb, n_ctx_chunks, rev, o_fwd=None):
    _, B, Lt, W = uc.shape
    C = DN_CHUNK
    nc = Lt // C
    nb = DN_STATE_BATCH
    d = 1 if rev else 0
    if rev:
        cidx = lambda i: jnp.where(i < n_ctx_chunks, n_ctx_chunks - 1 - i, nc - 1 - (i - n_ctx_chunks))
    else:
        cidx = lambda i: i
    tok = lambda w: pl.BlockSpec((nb, C, w), lambda b, i: (b, cidx(i), 0))
    in_specs = [pl.BlockSpec((1, nb, C, W), lambda b, i: (d, b, cidx(i), 0)),
                pl.BlockSpec((1, nb, 1, 2 * C, W), lambda b, i: (d, b, cidx(i), 0, 0)),
                pl.BlockSpec((1, nb, 1, DN_PAIRS, C + DN_DK, 2 * C), lambda b, i: (d, b, cidx(i), 0, 0, 0)),
                tok(LANES)]
    args = [uc, wq, qkkd, gb]
    if rev:
        in_specs.append(tok(W))
        args.append(o_fwd)
    return pl.pallas_call(
        functools.partial(_dn_state_body, rev=rev),
        out_shape=jax.ShapeDtypeStruct((B, Lt, W), f32),
        grid=(B // nb, nc),
        in_specs=in_specs,
        out_specs=tok(W),
        scratch_shapes=[pltpu.VMEM((nb, DN_HEADS, DN_DK, DN_DV), f32)],
        compiler_params=_cparams(("arbitrary", "arbitrary")),
        name="dn_state_bwd" if rev else "dn_state_fwd",
    )(*args)


def _attn_body(q_ref, kp_ref, kc_ref, kn_ref, kx_ref, vp_ref, vc_ref, vn_ref, vx_ref, sink_ref, o_ref,
               *, n_ctx_blocks, seq_len):
    Tq = AT_BLOCK
    G = AT_HEADS // AT_KV
    i = pl.program_id(1)
    is_ctx = i < n_ctx_blocks
    nloc = 3 * Tq
    qpos = (i - n_ctx_blocks) * Tq + lax.broadcasted_iota(jnp.int32, (Tq, nloc), 0)
    kpos = (i - n_ctx_blocks - 1) * Tq + lax.broadcasted_iota(jnp.int32, (Tq, nloc), 1)
    ok = jnp.logical_and(jnp.abs(qpos - kpos) <= WINDOW, jnp.logical_and(kpos >= 0, kpos < seq_len))
    ok = jnp.logical_and(ok, jnp.logical_not(is_ctx))
    k_all = jnp.concatenate([kp_ref[0], kc_ref[0], kn_ref[0], kx_ref[0]], axis=0).astype(bf16)
    v_all = jnp.concatenate([vp_ref[0], vc_ref[0], vn_ref[0], vx_ref[0]], axis=0)
    lane_q = lax.broadcasted_iota(jnp.int32, (Tq, LANES), 1)
    lane_v = lax.broadcasted_iota(jnp.int32, v_all.shape, 1)
    ug = AT_UNIT_GROUPS
    units = [(hk, gp) for hk in range(AT_KV) for gp in range(G // ug)]
    nt = (((1,), (1,)), ((), ()))
    ok_u = jnp.concatenate([ok] * ug, axis=0)
    lane_u = jnp.concatenate([lane_q] * ug, axis=0)
    rhs = []
    for hk in range(AT_KV):
        mine = lane_v // AT_HD == hk
        rhs.append(jnp.concatenate([jnp.where(mine, v_all, 0.0), mine.astype(f32)], axis=1).astype(bf16))
    scores, sinks = [], []
    for hk, gp in units:
        gs = range(gp * ug, (gp + 1) * ug)
        qs = jnp.concatenate([jnp.where(lane_q // AT_HD == hk, q_ref[0, :, g * LANES:(g + 1) * LANES], 0.0)
                              for g in gs], axis=0).astype(bf16)
        s = lax.dot_general(qs, k_all, nt, preferred_element_type=f32)
        scores.append(jnp.concatenate([jnp.where(ok_u, s[:, :nloc], -jnp.inf), s[:, nloc:]], axis=1))
        sinks.append(jnp.concatenate([jnp.broadcast_to(sink_ref[0:1, hk * G + g:hk * G + g + 1], (Tq, 1))
                                      for g in gs], axis=0))
    maxes = [jnp.maximum(jnp.max(s, axis=-1, keepdims=True), sk) for s, sk in zip(scores, sinks)]
    probs = [jnp.exp(s - m).astype(bf16) for s, m in zip(scores, maxes)]
    outs = [jnp.dot(p, rhs[hk], preferred_element_type=f32) for p, (hk, gp) in zip(probs, units)]
    for gp in range(G // ug):
        num = jnp.zeros((ug * Tq, LANES), f32)
        den = jnp.zeros((ug * Tq, LANES), f32)
        for u, (hk, gg) in enumerate(units):
            if gg == gp:
                num = num + outs[u][:, 0:LANES]
                den = den + outs[u][:, LANES:2 * LANES] + jnp.where(lane_u // AT_HD == hk,
                                                                    jnp.exp(sinks[u] - maxes[u]), 0.0)
        o = num / den
        for n in range(ug):
            g = gp * ug + n
            o_ref[0, :, g * LANES:(g + 1) * LANES] = o[n * Tq:(n + 1) * Tq, :]


def _attention(aq, ak, av, sink_vec, n_ctx, seq_len):
    B, Lt, _ = aq.shape
    nb = Lt // AT_BLOCK
    n_ctx_blocks = n_ctx // AT_BLOCK
    blk = lambda w, f: pl.BlockSpec((1, AT_BLOCK, w), lambda b, i: (b, f(i), 0))
    lo, hi = n_ctx_blocks, nb - 1
    prev = lambda i: jnp.clip(i - 1, lo, hi)
    cur = lambda i: i
    nxt = lambda i: jnp.clip(i + 1, lo, hi)
    ctx_spec = pl.BlockSpec((1, n_ctx, AT_KW), lambda b, i: (b, 0, 0))
    body = functools.partial(_attn_body, n_ctx_blocks=n_ctx_blocks, seq_len=seq_len)
    return pl.pallas_call(
        body,
        out_shape=jax.ShapeDtypeStruct((B, Lt, AT_QW), f32),
        grid=(B, nb),
        in_specs=[blk(AT_QW, cur),
                  blk(AT_KW, prev), blk(AT_KW, cur), blk(AT_KW, nxt), ctx_spec,
                  blk(AT_KW, prev), blk(AT_KW, cur), blk(AT_KW, nxt), ctx_spec,
                  _const_spec((1, LANES))],
        out_specs=blk(AT_QW, cur),
        compiler_params=_cparams(("arbitrary", "arbitrary")),
        name="window_attn",
    )(aq, ak, ak, ak, ak, av, av, av, av, sink_vec)


def _merge_body(x_ref, mod_ref, g_ref, ys_ref, od_ref, z_ref, yc_ref,
                wg_ref, wglu_ref, ng_ref, wa_ref, wb_ref, wc_ref, wo_ref, o_ref):
    x = x_ref[0]
    h = _modnorm(x, g_ref[...], mod_ref[0, 1:2, :], mod_ref[0, 0:1, :]).astype(bf16)
    zs = jax.nn.gelu(ys_ref[...])
    ya = zs * jax.nn.sigmoid(_dot(zs, wglu_ref[...]))
    ng = ng_ref[...]
    parts = []
    for hh in range(DN_HEADS):
        hs = slice(hh * DN_DV, (hh + 1) * DN_DV)
        o = od_ref[0, :, hs]
        on = o * lax.rsqrt(jnp.mean(o * o, axis=-1, keepdims=True) + EPS) * ng
        parts.append(on * _silu(z_ref[0, :, hs]))
    yb = jnp.concatenate(parts, axis=1)
    yc = yc_ref[0]
    D = D_MODEL

    def gate(j):
        return jax.nn.sigmoid(jnp.dot(h, wg_ref[:, j * D:(j + 1) * D], preferred_element_type=f32))

    m = gate(0) * _dot(ya, wa_ref[...])
    m = m + gate(1) * _dot(yb, wb_ref[...])
    m = m + gate(2) * _dot(yc, wc_ref[...])
    mix = _dot(m, wo_ref[...])
    o_ref[0] = x + mod_ref[0, 2:3, :] * mix


def _merge(xcat, mods, g1, ys_tm, o_dn, dz, yc, w_gate, w_glu, dn_g, w_a, w_b, w_c, w_o, n_ctx_tiles):
    B, Lt, D = xcat.shape
    nt = Lt // TOK_TILE
    tok = lambda w: pl.BlockSpec((1, TOK_TILE, w), lambda b, t: (b, t, 0))
    return pl.pallas_call(
        _merge_body,
        out_shape=jax.ShapeDtypeStruct((B, Lt, D), f32),
        grid=(B, nt),
        in_specs=[tok(D),
                  pl.BlockSpec((1, N_MOD, D), _mod_index(n_ctx_tiles, B)),
                  _const_spec((1, D)),
                  pl.BlockSpec((TOK_TILE, SSM_WIDTH), lambda b, t: (t, b)),
                  tok(DN_W), tok(DN_W), tok(AT_QW),
                  _const_spec(w_gate.shape), _const_spec(w_glu.shape), _const_spec((1, DN_DV)),
                  _const_spec(w_a.shape), _const_spec(w_b.shape), _const_spec(w_c.shape),
                  _const_spec(w_o.shape)],
        out_specs=tok(D),
        compiler_params=_cparams(("arbitrary", "arbitrary")),
        name="merge_out",
    )(xcat, mods, g1, ys_tm, o_dn, dz, yc, w_gate, w_glu, dn_g, w_a, w_b, w_c, w_o)


FF_CHUNK = 1024


def _ffn_body(x_ref, mod_ref, g_ref, w1_ref, w2_ref, fg_ref, o_ref):
    x = x_ref[0]
    h = _modnorm(x, g_ref[...], mod_ref[0, 4:5, :], mod_ref[0, 3:4, :]).astype(bf16)
    acc = jnp.zeros(x.shape, f32)
    for j in range(D_FF // FF_CHUNK):
        a = jnp.dot(h, w1_ref[:, j * FF_CHUNK:(j + 1) * FF_CHUNK], preferred_element_type=f32)
        a = jnp.square(jnp.maximum(a, 0.0)).astype(bf16)
        acc = acc + jnp.dot(a, w2_ref[j * FF_CHUNK:(j + 1) * FF_CHUNK, :], preferred_element_type=f32)
    y = x + mod_ref[0, 5:6, :] * acc
    if fg_ref is not None:
        y = y * lax.rsqrt(jnp.mean(y * y, axis=-1, keepdims=True) + EPS) * fg_ref[...]
    o_ref[0] = y


def _ffn_mid_body(x_ref, mod_ref, g_ref, w1_ref, w2_ref, o_ref):
    _ffn_body(x_ref, mod_ref, g_ref, w1_ref, w2_ref, None, o_ref)


def _ffn(xcat, mods, g2, w1, w2, n_ctx_tiles, final_g=None):
    B, Lt, D = xcat.shape
    nt = Lt // TOK_TILE
    tok = pl.BlockSpec((1, TOK_TILE, D), lambda b, t: (b, t, 0))
    weights = [_const_spec((1, D)), _const_spec(w1.shape), _const_spec(w2.shape)]
    if final_g is None:
        return pl.pallas_call(
            _ffn_mid_body,
            out_shape=jax.ShapeDtypeStruct((B, Lt, D), f32),
            grid=(B, nt),
            in_specs=[tok, pl.BlockSpec((1, N_MOD, D), _mod_index(n_ctx_tiles, B))] + weights,
            out_specs=tok,
            compiler_params=_cparams(("arbitrary", "arbitrary")),
            name="ffn",
        )(xcat, mods, g2, w1, w2)
    return pl.pallas_call(
        _ffn_body,
        out_shape=jax.ShapeDtypeStruct((B, Lt - n_ctx_tiles * TOK_TILE, D), f32),
        grid=(B, nt - n_ctx_tiles),
        in_specs=[pl.BlockSpec((1, TOK_TILE, D), lambda b, t: (b, t + n_ctx_tiles, 0)),
                  pl.BlockSpec((1, N_MOD, D), lambda b, t: (b, 0, 0))] + weights + [_const_spec((1, D))],
        out_specs=tok,
        compiler_params=_cparams(("arbitrary", "arbitrary")),
        name="ffn_final",
    )(xcat, mods, g2, w1, w2, final_g)


def _rope_tables(n_ctx, seq_len):
    n = AT_HD // 4
    inv_freq = ROPE_BASE ** (-jnp.arange(n, dtype=f32) / n)
    pos = jnp.arange(seq_len, dtype=jnp.int32)
    rows = (pos // GRID_W).astype(f32)[:, None] * inv_freq[None, :]
    cols = (pos % GRID_W).astype(f32)[:, None] * inv_freq[None, :]
    cos = jnp.concatenate([jnp.cos(rows)] * 2 + [jnp.cos(cols)] * 2, axis=1)
    sin = jnp.concatenate([-jnp.sin(rows), jnp.sin(rows), -jnp.sin(cols), jnp.sin(cols)], axis=1)
    cos = jnp.concatenate([jnp.ones((n_ctx, AT_HD), f32), cos], axis=0)
    sin = jnp.concatenate([jnp.zeros((n_ctx, AT_HD), f32), sin], axis=0)
    return jnp.tile(cos, (1, LANES // AT_HD)), jnp.tile(sin, (1, LANES // AT_HD))


def _attn_head_perm():
    G = AT_HEADS // AT_KV
    n = jnp.arange(AT_QW)
    g, hk, e = n // LANES, (n % LANES) // AT_HD, n % AT_HD
    return (hk * G + g) * AT_HD + e


def _split_w_in(w_in):
    o = SSM_WIDTH + 4 * DN_W
    w_ba = w_in[:, o:o + 4 * DN_HEADS]
    o2 = o + 4 * DN_HEADS
    w_att = w_in[:, o2:o2 + AT_QW + 2 * AT_KW]
    w_att = jnp.concatenate([w_att[:, :AT_QW][:, _attn_head_perm()], w_att[:, AT_QW:]], axis=1)
    w_gate = w_in[:, o2 + AT_QW + 2 * AT_KW:]
    w_main = jnp.concatenate([w_in[:, :o], w_att], axis=1).astype(bf16)
    w_ba = jnp.pad(w_ba, ((0, 0), (0, LANES - 4 * DN_HEADS))).astype(bf16)
    return w_main, w_ba, w_gate.astype(bf16)


def kernel(x, c, ctx, c_ctx, norm1_g, norm2_g, w_mod, b_mod, w_in, ssm_lam_re, ssm_lam_im, ssm_log_dt, ssm_b_re, ssm_b_im, ssm_c_re, ssm_c_im, ssm_d, ssm_w_glu, dn_conv_w, dn_a_log, dn_dt_bias, dn_norm_g, attn_sink, w_branch_a, w_branch_b, w_branch_c, w_out, w_ff1, w_ff2, final_norm_g):
    B, L, D = x.shape
    Lc = ctx.shape[1]
    depth = w_in.shape[0]
    assert B == SUBLANES and D == D_MODEL
    assert Lc % TOK_TILE == 0 and L % TOK_TILE == 0
    Lt = Lc + L
    n_ctx_tiles = Lc // TOK_TILE

    cond = jnp.concatenate([c, c_ctx[None, :], jnp.zeros((16 - B - 1, D), f32)], axis=0)
    mods_all = _modulation(cond, w_mod, b_mod).reshape(depth, 16, N_MOD, D)
    cos_t, sin_t = _rope_tables(Lc, L)
    xcat = jnp.concatenate([ctx, x], axis=1)

    for layer in range(depth):
        mods = mods_all[layer]
        g1 = norm1_g[layer].reshape(1, D)
        g2 = norm2_g[layer].reshape(1, D)
        w_main, w_ba, w_gate = _split_w_in(w_in[layer])

        conv_w = jnp.pad(dn_conv_w[layer], ((0, SUBLANES - DN_CONV), (0, 0)))
        pad_l = 2 * DN_HEADS
        al_vec = jnp.pad(dn_a_log[layer].reshape(1, 2 * DN_HEADS), ((0, 0), (pad_l, LANES - 2 * pad_l)))
        dtb_vec = jnp.pad(dn_dt_bias[layer].reshape(1, 2 * DN_HEADS), ((0, 0), (pad_l, LANES - 2 * pad_l)))
        u_tm, dz, aq, ak, av, qn, kn, vs, gb, gbt = _inproj(xcat, mods, g1, w_main, w_ba, cos_t, sin_t,
                                                            conv_w, al_vec, dtb_vec, n_ctx_tiles)

        a_re, a_im, bbt_re, bbt_im = _s5_discretize(ssm_lam_re[layer], ssm_lam_im[layer], ssm_log_dt[layer],
                                                    ssm_b_re[layer], ssm_b_im[layer])
        bmat, cmat, avec = _s5_matrices(a_re, a_im, bbt_re, bbt_im, ssm_c_re[layer], ssm_c_im[layer])
        u_rows = u_tm.reshape(Lt * B, SSM_WIDTH)
        y_f = _s5_scan(u_rows, bmat, cmat, avec, Lc, rev=False)
        y_s5 = _s5_scan(u_rows, bmat, cmat, avec, Lc, rev=True, yf=y_f,
                        d_skip=ssm_d[layer].reshape(1, SSM_WIDTH))
        ys_tm = y_s5.reshape(Lt, B * SSM_WIDTH)

        n_ctx_chunks = Lc // DN_CHUNK
        uc, wq, qkkd = _dn_intra(qn, kn, vs, gb, gbt)
        o_f = _dn_state(uc, wq, qkkd, gb, n_ctx_chunks, rev=False)
        o_dn = _dn_state(uc, wq, qkkd, gb, n_ctx_chunks, rev=True, o_fwd=o_f)

        sink_vec = jnp.pad(attn_sink[layer].reshape(1, AT_HEADS), ((0, 0), (0, LANES - AT_HEADS)))
        yc = _attention(aq, ak, av, sink_vec, Lc, L)

        x1 = _merge(xcat, mods, g1, ys_tm, o_dn, dz, yc, w_gate, ssm_w_glu[layer].astype(bf16),
                    dn_norm_g[layer].reshape(1, DN_DV), w_branch_a[layer].astype(bf16),
                    w_branch_b[layer].astype(bf16), w_branch_c[layer][_attn_head_perm(), :].astype(bf16),
                    w_out[layer].astype(bf16), n_ctx_tiles)
        final_g = final_norm_g.reshape(1, D) if layer == depth - 1 else None
        xcat = _ffn(x1, mods, g2, w_ff1[layer].astype(bf16), w_ff2[layer].astype(bf16), n_ctx_tiles, final_g)

    return xcat
```

```python
import functools
import math

import jax
import jax.numpy as jnp
from jax import lax
from jax.experimental import pallas as pl
from jax.experimental.pallas import tpu as pltpu

f32 = jnp.float32
bf16 = jnp.bfloat16
HI = lax.Precision.HIGHEST

D_MODEL = 1024
GRID_W = 64
EPS = 1e-6
SSM_WIDTH = D_MODEL // 2
SSM_GROUP = 16
SSM_GROUPS = SSM_WIDTH // SSM_GROUP
SSM_STATE = 64
DN_HEADS = 4
DN_DK = 128
DN_DV = 128
DN_CONV = 5
DN_CHUNK = 64
AT_HEADS = 8
AT_KV = 2
AT_HD = 64
WINDOW = 128
AT_BLOCK = 128
AT_UNIT_GROUPS = 1
ROPE_BASE = 10000.0
D_FF = 4 * D_MODEL
N_MOD = 6
DN_W = DN_HEADS * DN_DK
AT_QW = AT_HEADS * AT_HD
AT_KW = AT_KV * AT_HD

SUBLANES = 8
LANES = 128
TOK_TILE = 256
S5_STEPS = 128
S5_COLBLK = 128
S5_NBLK = SSM_WIDTH // S5_COLBLK
S5_SBLK = (S5_COLBLK // SSM_GROUP) * SSM_STATE
VMEM_LIMIT = 56 * 1024 * 1024


def _cparams(sem):
    return pltpu.CompilerParams(dimension_semantics=sem, vmem_limit_bytes=VMEM_LIMIT)


def _const_spec(shape):
    nd = len(shape)
    return pl.BlockSpec(shape, lambda *_: (0,) * nd, pipeline_mode=pl.Buffered(1))


def _dot(a, b):
    return jnp.dot(a.astype(bf16), b.astype(bf16), preferred_element_type=f32)


def _dot_nt(a, b):
    return lax.dot_general(a.astype(bf16), b.astype(bf16), (((1,), (1,)), ((), ())),
                           preferred_element_type=f32)


def _dot_tn(a, b):
    return lax.dot_general(a.astype(bf16), b.astype(bf16), (((0,), (0,)), ((), ())),
                           preferred_element_type=f32)


def _modnorm(x, g, scale, shift):
    y = x * lax.rsqrt(jnp.mean(x * x, axis=-1, keepdims=True) + EPS)
    return (y * g) * (1.0 + scale) + shift


def _silu(x):
    return x * jax.nn.sigmoid(x)


def _bf16_pieces(x):
    hi = x.astype(bf16)
    r = x - hi.astype(f32)
    mid = r.astype(bf16)
    lo = (r - mid.astype(f32)).astype(bf16)
    return [hi, mid, lo]


def _mod_body(c_ref, w_ref, b_ref, o_ref):
    s = _silu(c_ref[...])
    o_ref[0] = jnp.dot(s, w_ref[0], precision=HI, preferred_element_type=f32) + b_ref[0]


def _modulation(cond, w_mod, b_mod):
    depth = w_mod.shape[0]
    nblk = (N_MOD * D_MODEL) // D_MODEL
    return pl.pallas_call(
        _mod_body,
        out_shape=jax.ShapeDtypeStruct((depth, 16, N_MOD * D_MODEL), f32),
        grid=(depth, nblk),
        in_specs=[pl.BlockSpec((16, D_MODEL), lambda l, j: (0, 0)),
                  pl.BlockSpec((1, D_MODEL, D_MODEL), lambda l, j: (l, 0, j)),
                  pl.BlockSpec((1, 1, D_MODEL), lambda l, j: (l, 0, j))],
        out_specs=pl.BlockSpec((1, 16, D_MODEL), lambda l, j: (l, 0, j)),
        compiler_params=_cparams(("arbitrary", "arbitrary")),
        name="adaln_mod",
    )(cond, w_mod, b_mod.reshape(depth, 1, N_MOD * D_MODEL))


def _rope(x, cos, sin):
    n = x.shape[-1]
    lane = lax.broadcasted_iota(jnp.int32, x.shape, 1)
    partner = jnp.where((lane % 32) < 16, pltpu.roll(x, n - 16, 1), pltpu.roll(x, 16, 1))
    return x * cos + partner * sin


def _inproj_step(x_ref, mod_ref, g_ref, wm_ref, wba_ref, cos_ref, sin_ref, cw_ref, al_ref, dtb_ref,
                 u_ref, z_ref, aq_ref, ak_ref, av_ref, q_ref, k_ref, v_ref, gb_ref, gbt_ref,
                 qkv_scr, ba_scr, halo_scr, *, cur, prv, n_ctx_tiles, n_tiles):
    T = TOK_TILE
    t = pl.program_id(1)
    h = _modnorm(x_ref[0], g_ref[...], mod_ref[0, 1:2, :], mod_ref[0, 0:1, :]).astype(bf16)

    def proj(lo, hi):
        return jnp.dot(h, wm_ref[:, lo:hi], preferred_element_type=f32)

    o_u = 0
    o_qkv = o_u + SSM_WIDTH
    o_z = o_qkv + 3 * DN_W
    o_aq = o_z + DN_W
    o_ak = o_aq + AT_QW
    o_av = o_ak + AT_KW
    qkv_scr[cur, SUBLANES:SUBLANES + T, :] = proj(o_qkv, o_z)

    tp = t - 1
    prev_ok = jnp.logical_and(tp != 0, tp != n_ctx_tiles)
    next_ok = jnp.logical_and(tp != n_ctx_tiles - 1, tp != n_tiles - 1)
    qkv_scr[prv, 0:SUBLANES, :] = jnp.where(prev_ok, halo_scr[...], 0.0)
    qkv_scr[prv, SUBLANES + T:2 * SUBLANES + T, :] = jnp.where(next_ok, qkv_scr[cur, SUBLANES:2 * SUBLANES, :], 0.0)
    halo_scr[...] = qkv_scr[prv, T:T + SUBLANES, :]
    half = DN_CONV // 2

    def conv_act(cs):
        acc = cw_ref[0:1, cs] * qkv_scr[prv, pl.ds(SUBLANES - half, T), cs]
        for kk in range(1, DN_CONV):
            acc = acc + cw_ref[kk:kk + 1, cs] * qkv_scr[prv, pl.ds(SUBLANES - half + kk, T), cs]
        return _silu(acc)

    def prep_q(hh):
        hs = slice(hh * DN_DK, (hh + 1) * DN_DK)
        qs = conv_act(hs)
        q_ref[0, :, hs] = qs * (lax.rsqrt(jnp.sum(qs * qs, axis=-1, keepdims=True) + EPS) * (DN_DK ** -0.5))

    def prep_k(hh):
        hs = slice(hh * DN_DK, (hh + 1) * DN_DK)
        ks = conv_act(slice(DN_W + hh * DN_DK, DN_W + (hh + 1) * DN_DK))
        k_ref[0, :, hs] = ks * lax.rsqrt(jnp.sum(ks * ks, axis=-1, keepdims=True) + EPS)

    def prep_v(hh):
        hs = slice(hh * DN_DV, (hh + 1) * DN_DV)
        v_ref[0, :, hs] = conv_act(slice(2 * DN_W + hh * DN_DV, 2 * DN_W + (hh + 1) * DN_DV))

    cos = cos_ref[...]
    sin = sin_ref[...]
    reps = AT_QW // LANES
    hw = SSM_WIDTH // 2

    def proj_u(n):
        u_ref[:, n * hw:(n + 1) * hw] = proj(o_u + n * hw, o_u + (n + 1) * hw)

    def proj_z(n):
        z_ref[0, :, n * hw:(n + 1) * hw] = proj(o_z + n * hw, o_z + (n + 1) * hw)

    def proj_aq():
        aq_ref[0] = _rope(proj(o_aq, o_ak), jnp.concatenate([cos] * reps, axis=1),
                          jnp.concatenate([sin] * reps, axis=1)) * (AT_HD ** -0.5)

    def proj_akv():
        ak_ref[0] = _rope(proj(o_ak, o_av), cos, sin)
        av_ref[0] = proj(o_av, o_av + AT_KW)
        ba_scr[cur] = jnp.dot(h, wba_ref[...], preferred_element_type=f32)

    mxu_work = [lambda: proj_u(0), lambda: proj_u(1), lambda: proj_z(0), lambda: proj_z(1), proj_aq, proj_akv]
    vpu_work = ([functools.partial(prep_q, hh) for hh in range(DN_HEADS)]
                + [functools.partial(prep_k, hh) for hh in range(DN_HEADS)]
                + [functools.partial(prep_v, hh) for hh in range(DN_HEADS)])
    per = len(vpu_work) // len(mxu_work)
    for n, mx in enumerate(mxu_work):
        mx()
        for vp in vpu_work[n * per:(n + 1) * per]:
            vp()

    ba = ba_scr[prv]
    lane = lax.broadcasted_iota(jnp.int32, ba.shape, 1)
    beta = jax.nn.sigmoid(ba)
    g = -jnp.exp(al_ref[...]) * jax.nn.softplus(ba + dtb_ref[...])
    ri = lax.broadcasted_iota(jnp.int32, (T, T), 0)
    ci = lax.broadcasted_iota(jnp.int32, (T, T), 1)
    same = (ri // DN_CHUNK) == (ci // DN_CHUNK)
    tri_f = jnp.logical_and(same, ri >= ci).astype(bf16)
    tri_b = jnp.logical_and(same, ri <= ci).astype(bf16)
    g3 = jnp.concatenate(_bf16_pieces(g), axis=0)
    gc_f = jnp.dot(jnp.concatenate([tri_f] * 3, axis=1), g3, preferred_element_type=f32)
    gc_b = jnp.dot(jnp.concatenate([tri_b] * 3, axis=1), g3, preferred_element_type=f32)
    nh = DN_HEADS
    gc = jnp.where(lane < 2 * nh + nh, gc_f, gc_b)
    gb = jnp.where(lane < 2 * nh, beta, jnp.where(lane < 4 * nh, gc, 0.0))
    gb_ref[0] = gb
    er = lax.broadcasted_iota(jnp.int32, (16, 3 * LANES), 0)
    ec = lax.broadcasted_iota(jnp.int32, (16, 3 * LANES), 1)
    sel = (er == ec % LANES).astype(bf16)
    gbt = lax.dot_general(sel, jnp.concatenate(_bf16_pieces(gb), axis=1), (((1,), (1,)), ((), ())),
                          preferred_element_type=f32)
    for c in range(T // DN_CHUNK):
        gbt_ref[0, c] = gbt[:, c * DN_CHUNK:(c + 1) * DN_CHUNK]


def _inproj_body(*refs, n_ctx_tiles, n_tiles):
    qkv_scr, ba_scr, halo_scr = refs[-3:]
    t = pl.program_id(1)

    @pl.when(jnp.logical_and(pl.program_id(0) == 0, t == 0))
    def _():
        qkv_scr[...] = jnp.zeros_like(qkv_scr)
        ba_scr[...] = jnp.zeros_like(ba_scr)
        halo_scr[...] = jnp.zeros_like(halo_scr)

    for parity in range(2):
        @pl.when(jnp.bitwise_and(t, 1) == parity)
        def _(parity=parity):
            _inproj_step(*refs, cur=parity, prv=1 - parity, n_ctx_tiles=n_ctx_tiles, n_tiles=n_tiles)


def _mod_index(n_ctx_tiles, ctx_row):
    return lambda b, t: (jnp.where(t < n_ctx_tiles, ctx_row, b), 0, 0)


def _inproj(xcat, mods, g1, w_main, w_ba, cos_t, sin_t, conv_w, al_vec, dtb_vec, n_ctx_tiles):
    B, Lt, D = xcat.shape
    nt = Lt // TOK_TILE
    wm = w_main.shape[1]
    tc = lambda t: jnp.minimum(t, nt - 1)
    tp = lambda t: jnp.maximum(t - 1, 0)
    cur = lambda w: pl.BlockSpec((1, TOK_TILE, w), lambda b, t: (b, tc(t), 0))
    prev = lambda w: pl.BlockSpec((1, TOK_TILE, w), lambda b, t: (b, tp(t), 0))
    mod_idx = _mod_index(n_ctx_tiles, B)
    out_shape = (
        jax.ShapeDtypeStruct((Lt, B * SSM_WIDTH), f32),
        jax.ShapeDtypeStruct((B, Lt, DN_W), f32),
        jax.ShapeDtypeStruct((B, Lt, AT_QW), f32),
        jax.ShapeDtypeStruct((B, Lt, AT_KW), f32),
        jax.ShapeDtypeStruct((B, Lt, AT_KW), f32),
        jax.ShapeDtypeStruct((B, Lt, DN_W), f32),
        jax.ShapeDtypeStruct((B, Lt, DN_W), f32),
        jax.ShapeDtypeStruct((B, Lt, DN_W), f32),
        jax.ShapeDtypeStruct((B, Lt, LANES), f32),
        jax.ShapeDtypeStruct((B, Lt // DN_CHUNK, 16, DN_CHUNK), f32),
    )
    body = functools.partial(_inproj_body, n_ctx_tiles=n_ctx_tiles, n_tiles=nt)
    return pl.pallas_call(
        body,
        out_shape=out_shape,
        grid=(B, nt + 1),
        in_specs=[cur(D),
                  pl.BlockSpec((1, N_MOD, D), lambda b, t: mod_idx(b, tc(t))),
                  _const_spec((1, D)),
                  _const_spec((D, wm)),
                  _const_spec((D, LANES)),
                  pl.BlockSpec((TOK_TILE, LANES), lambda b, t: (tc(t), 0)),
                  pl.BlockSpec((TOK_TILE, LANES), lambda b, t: (tc(t), 0)),
                  _const_spec((SUBLANES, 3 * DN_W)),
                  _const_spec((1, LANES)),
                  _const_spec((1, LANES))],
        out_specs=(pl.BlockSpec((TOK_TILE, SSM_WIDTH), lambda b, t: (tc(t), b)),
                   cur(DN_W), cur(AT_QW), cur(AT_KW), cur(AT_KW),
                   prev(DN_W), prev(DN_W), prev(DN_W), prev(LANES),
                   pl.BlockSpec((1, TOK_TILE // DN_CHUNK, 16, DN_CHUNK), lambda b, t: (b, tp(t), 0, 0))),
        scratch_shapes=[pltpu.VMEM((2, TOK_TILE + 2 * SUBLANES, 3 * DN_W), f32),
                        pltpu.VMEM((2, TOK_TILE, LANES), f32),
                        pltpu.VMEM((SUBLANES, 3 * DN_W), f32)],
        compiler_params=_cparams(("arbitrary", "arbitrary")),
        name="in_proj",
    )(xcat, mods, g1, w_main, w_ba, cos_t, sin_t, conv_w, al_vec, dtb_vec)


def _s5disc_body(lre_ref, lim_ref, ldt_ref, bre_ref, bim_ref, are_ref, aim_ref, bbre_ref, bbim_ref):
    lr = lre_ref[0]
    li = lim_ref[0]
    dt = jnp.exp(ldt_ref[0])
    mag = jnp.exp(lr * dt)
    a_re = mag * jnp.cos(li * dt)
    a_im = mag * jnp.sin(li * dt)
    den = lr * lr + li * li
    f_re = ((a_re - 1.0) * lr + a_im * li) / den
    f_im = (a_im * lr - (a_re - 1.0) * li) / den
    are_ref[0] = a_re
    aim_ref[0] = a_im
    b_re = bre_ref[0]
    b_im = bim_ref[0]
    bbre_ref[0] = f_re * b_re - f_im * b_im
    bbim_ref[0] = f_re * b_im + f_im * b_re


def _s5_discretize(lam_re, lam_im, log_dt, b_re, b_im):
    n = 2 * SSM_GROUPS
    P, H = SSM_STATE, SSM_GROUP
    vec = pl.BlockSpec((1, 1, P), lambda i: (i, 0, 0))
    mat = pl.BlockSpec((1, H, P), lambda i: (i, 0, 0))
    return pl.pallas_call(
        _s5disc_body,
        out_shape=(jax.ShapeDtypeStruct((n, 1, P), f32), jax.ShapeDtypeStruct((n, 1, P), f32),
                   jax.ShapeDtypeStruct((n, H, P), f32), jax.ShapeDtypeStruct((n, H, P), f32)),
        grid=(n,),
        in_specs=[vec, vec, pl.BlockSpec((1, 1, 1), lambda i: (i, 0, 0)), mat, mat],
        out_specs=(vec, vec, mat, mat),
        compiler_params=_cparams(("arbitrary",)),
        name="s5_discretize",
    )(lam_re.reshape(n, 1, P), lam_im.reshape(n, 1, P), log_dt.reshape(n, 1, 1),
      jnp.swapaxes(b_re, -1, -2).reshape(n, H, P), jnp.swapaxes(b_im, -1, -2).reshape(n, H, P))


def _s5_matrices(a_re, a_im, bbt_re, bbt_im, c_re, c_im):
    gb = S5_COLBLK // SSM_GROUP
    P, H = SSM_STATE, SSM_GROUP
    eye = jnp.eye(gb, dtype=f32)

    def bblk(t):
        t = t.reshape(2, S5_NBLK, gb, H, P)
        return jnp.einsum('dcghp,gk->dcghkp', t, eye).reshape(2, S5_NBLK, gb * H, gb * P)

    def cblk(t):
        t = t.reshape(2, S5_NBLK, gb, H, P)
        return jnp.einsum('dcghp,gk->dcgpkh', t, eye).reshape(2, S5_NBLK, gb * P, gb * H)

    bmat = jnp.concatenate([bblk(bbt_re), bblk(bbt_im)], axis=-1).astype(bf16)
    cmat = jnp.concatenate([cblk(c_re.astype(f32)), -cblk(c_im.astype(f32))], axis=-2).astype(bf16)
    avec = jnp.stack([a_re.reshape(2, S5_NBLK, gb * P), a_im.reshape(2, S5_NBLK, gb * P)], axis=2)
    return bmat, cmat, avec


def _s5_scan_body(*refs, rev):
    if rev:
        u_ref, bm_ref, cm_ref, a_ref, yf_ref, d_ref, y_ref, bu_scr, st_scr = refs
    else:
        u_ref, bm_ref, cm_ref, a_ref, y_ref, bu_scr, st_scr = refs
    B = SUBLANES
    nsb = S5_SBLK

    @pl.when(pl.program_id(0) == 0)
    def _():
        st_scr[...] = jnp.zeros_like(st_scr)

    def tile_rows(t):
        return pl.ds(((S5_STEPS - 1 - t) if rev else t) * B, B)

    for cb in range(S5_NBLK):
        cols = slice(cb * S5_COLBLK, (cb + 1) * S5_COLBLK)
        if rev:
            u_blk = jnp.concatenate([u_ref[tile_rows(t), cols] for t in range(S5_STEPS)], axis=0)
        else:
            u_blk = u_ref[:, cols]
        bu_scr[cb] = jnp.dot(u_blk.astype(bf16), bm_ref[0, cb], preferred_element_type=f32)
        are = jnp.broadcast_to(a_ref[0, cb, 0:1, :], (B, nsb))
        aim = jnp.broadcast_to(a_ref[0, cb, 1:2, :], (B, nsb))
        sre = st_scr[cb, :, 0:nsb]
        sim = st_scr[cb, :, nsb:2 * nsb]
        for t in range(S5_STEPS):
            rows = pl.ds(t * B, B)
            nre = are * sre - aim * sim + bu_scr[cb, rows, 0:nsb]
            nim = are * sim + aim * sre + bu_scr[cb, rows, nsb:2 * nsb]
            bu_scr[cb, rows, 0:nsb] = nre
            bu_scr[cb, rows, nsb:2 * nsb] = nim
            sre, sim = nre, nim
        st_scr[cb, :, 0:nsb] = sre
        st_scr[cb, :, nsb:2 * nsb] = sim
        y = jnp.dot(bu_scr[cb].astype(bf16), cm_ref[0, cb], preferred_element_type=f32)
        if rev:
            for t in range(S5_STEPS):
                rows = tile_rows(t)
                y_ref[rows, cols] = (y[t * B:(t + 1) * B, :] + yf_ref[rows, cols]
                                     + d_ref[:, cols] * u_ref[rows, cols])
        else:
            y_ref[:, cols] = y


def _s5_tile_index(n_ctx_tiles, n_tiles, rev):
    if not rev:
        return lambda i: (i, 0)
    return lambda i: (jnp.where(i < n_ctx_tiles, n_ctx_tiles - 1 - i, n_tiles - 1 - (i - n_ctx_tiles)), 0)


def _s5_scan(u_rows, bmat, cmat, avec, n_ctx_steps, rev, yf=None, d_skip=None):
    n_rows = u_rows.shape[0]
    rows = S5_STEPS * SUBLANES
    n_tiles = n_rows // rows
    n_ctx_tiles = n_ctx_steps // S5_STEPS
    idx = _s5_tile_index(n_ctx_tiles, n_tiles, rev)
    d = 1 if rev else 0
    tile = pl.BlockSpec((rows, SSM_WIDTH), idx)
    in_specs = [tile,
                pl.BlockSpec((1, S5_NBLK, S5_COLBLK, 2 * S5_SBLK), lambda i: (d, 0, 0, 0), pipeline_mode=pl.Buffered(1)),
                pl.BlockSpec((1, S5_NBLK, 2 * S5_SBLK, S5_COLBLK), lambda i: (d, 0, 0, 0), pipeline_mode=pl.Buffered(1)),
                pl.BlockSpec((1, S5_NBLK, 2, S5_SBLK), lambda i: (d, 0, 0, 0), pipeline_mode=pl.Buffered(1))]
    args = [u_rows, bmat, cmat, avec]
    if rev:
        in_specs += [tile, _const_spec((1, SSM_WIDTH))]
        args += [yf, d_skip]
    return pl.pallas_call(
        functools.partial(_s5_scan_body, rev=rev),
        out_shape=jax.ShapeDtypeStruct((n_rows, SSM_WIDTH), f32),
        grid=(n_tiles,),
        in_specs=in_specs,
        out_specs=tile,
        scratch_shapes=[pltpu.VMEM((S5_NBLK, rows, 2 * S5_SBLK), f32),
                        pltpu.VMEM((S5_NBLK, SUBLANES, 2 * S5_SBLK), f32)],
        compiler_params=_cparams(("arbitrary",)),
        name="s5_scan_bwd" if rev else "s5_scan_fwd",
    )(*args)


DN_HALF = DN_CHUNK // 2
DN_PAIRS = DN_HEADS // 2
DN_STATE_BATCH = 8
DN_INTRA_CHUNKS = 4


def _diag_block_inverses(n_scr, revs):
    H = DN_HALF
    ng = H // SUBLANES
    ri = lax.broadcasted_iota(jnp.int32, (SUBLANES, LANES), 0)
    ci = lax.broadcasted_iota(jnp.int32, (SUBLANES, LANES), 1)
    eye = [(ri + SUBLANES * g == ci % H).astype(f32) for g in range(ng)]
    base = (ci // H) * H
    ts = [list(eye) for _ in revs]
    for step in range(H - 1):
        for p, rev in enumerate(revs):
            j = H - 1 - step if rev else step
            gj = j // SUBLANES
            idx = base + j
            row = ts[p][gj][j % SUBLANES:j % SUBLANES + 1, :]
            for g in (range(0, gj + 1) if rev else range(gj, ng)):
                col = jnp.take_along_axis(n_scr[p, SUBLANES * g:SUBLANES * (g + 1), :], idx, axis=1)
                ts[p][g] = ts[p][g] - col * row
    return [jnp.concatenate(t, axis=0) for t in ts]


def _split3_dot(a, b):
    a_hi = a.astype(bf16)
    a_lo = (a - a_hi.astype(f32)).astype(bf16)
    b_hi = b.astype(bf16)
    b_lo = (b - b_hi.astype(f32)).astype(bf16)
    return jnp.dot(jnp.concatenate([a_hi, a_hi, a_lo], axis=1), jnp.concatenate([b_hi, b_lo, b_hi], axis=0),
                   preferred_element_type=f32)


def _block_diag2(a, b):
    z = jnp.zeros_like(a)
    return jnp.concatenate([jnp.concatenate([a, z], axis=1), jnp.concatenate([z, b], axis=1)], axis=0)


def _dn_intra_body(q_ref, k_ref, v_ref, gb_ref, gbt_ref, uc_ref, wq_ref, qkkd_ref, np_scr, n_scr):
    C = DN_CHUNK
    H = DN_HALF
    ri = lax.broadcasted_iota(jnp.int32, (C, 2 * C), 0)
    ci = lax.broadcasted_iota(jnp.int32, (C, 2 * C), 1)
    cl = ci % C
    incl = (ri >= cl, ri <= cl)
    strict = (ri > cl, ri < cl)
    left = ci < C
    cih = lax.broadcasted_iota(jnp.int32, (H, 2 * C), 1)
    lo_h = (cih % C) < H
    left_h = cih < C
    zero_h = jnp.zeros((H, 2 * C), f32)
    last = (C - 1, 0)
    er = lax.broadcasted_iota(jnp.int32, (DN_DK, DN_DK), 0)
    ec = lax.broadcasted_iota(jnp.int32, (DN_DK, DN_DK), 1)
    eye_dk = (er == ec).astype(bf16)
    n_per = 2 * DN_PAIRS
    rows_of = [slice(cc * C, (cc + 1) * C) for cc in range(DN_INTRA_CHUNKS)]
    gbs = [gb_ref[0, rows_of[cc], :] for cc in range(DN_INTRA_CHUNKS)]

    for cc in range(DN_INTRA_CHUNKS):
        rows = rows_of[cc]
        gb = gbs[cc]
        for pr in range(DN_PAIRS):
            h0, h1 = 2 * pr, 2 * pr + 1
            s0 = slice(h0 * DN_DK, (h0 + 1) * DN_DK)
            s1 = slice(h1 * DN_DK, (h1 + 1) * DN_DK)
            k0 = k_ref[0, rows, s0].astype(bf16)
            k1 = k_ref[0, rows, s1].astype(bf16)
            kbd = _block_diag2(k0, k1)
            kk_p = lax.dot_general(jnp.concatenate([k0, k1], axis=1), kbd, (((1,), (1,)), ((), ())),
                                   preferred_element_type=f32)
            qcat = jnp.concatenate([q_ref[0, rows, s0], q_ref[0, rows, s1]], axis=1).astype(bf16)
            qk_p = lax.dot_general(qcat, kbd, (((1,), (1,)), ((), ())), preferred_element_type=f32)
            kt_p = lax.dot_general(eye_dk, jnp.concatenate([k0, k1], axis=0), (((1,), (1,)), ((), ())),
                                   preferred_element_type=f32)
            for d in range(2):
                bl0, bl1 = d * DN_HEADS + h0, d * DN_HEADS + h1
                gl0, gl1 = 2 * DN_HEADS + bl0, 2 * DN_HEADS + bl1
                beta_p = jnp.take_along_axis(gb, jnp.where(left, bl0, bl1), axis=1)
                gc_p = jnp.take_along_axis(gb, jnp.where(left, gl0, gl1), axis=1)
                gc_row = jnp.concatenate([gbt_ref[0, cc, gl0:gl0 + 1, :], gbt_ref[0, cc, gl1:gl1 + 1, :]], axis=1)
                decay = jnp.exp(jnp.where(incl[d], gc_p - gc_row, -jnp.inf))
                n_p = jnp.where(strict[d], (beta_p * kk_p) * decay, 0.0)
                p = cc * n_per + pr * 2 + d
                np_scr[p] = n_p
                n_scr[p] = jnp.where(lo_h, n_p[:H], n_p[H:])
                qkkd_ref[d, 0, cc, pr, 0:C, :] = jnp.where(incl[d], qk_p * decay, 0.0).astype(bf16)
                g_end = gc_p[last[d]:last[d] + 1, :]
                qkkd_ref[d, 0, cc, pr, C:C + DN_DK, :] = (kt_p * jnp.exp(g_end - gc_row)).astype(bf16)

    t_packs = _diag_block_inverses(n_scr, [False, True] * (DN_PAIRS * DN_INTRA_CHUNKS))

    def scaled(cc, hh, d):
        hs = slice(hh * DN_DK, (hh + 1) * DN_DK)
        rows = rows_of[cc]
        bl = d * DN_HEADS + hh
        full = jnp.zeros((C, LANES), jnp.int32)
        beta = jnp.take_along_axis(gbs[cc], full + bl, axis=1)
        egc = jnp.exp(jnp.take_along_axis(gbs[cc], full + (2 * DN_HEADS + bl), axis=1))
        wq_ref[d, 0, cc, C:2 * C, hs] = (q_ref[0, rows, hs] * egc).astype(bf16)
        return jnp.concatenate([v_ref[0, rows, hs] * beta, k_ref[0, rows, hs] * (beta * egc)], axis=1)

    packs = [(cc, pr, d) for cc in range(DN_INTRA_CHUNKS) for pr in range(DN_PAIRS) for d in range(2)]
    hi_h = jnp.logical_not(lo_h)
    right_h = jnp.logical_not(left_h)
    xs = []
    for cc, pr, d in packs:
        p = cc * n_per + pr * 2 + d
        tp = t_packs[p]
        if d == 0:
            nb = np_scr[p, H:C, :]
            r1 = jnp.concatenate([zero_h, jnp.where(jnp.logical_and(lo_h, left_h), nb, 0.0), zero_h,
                                  jnp.where(jnp.logical_and(lo_h, right_h), nb, 0.0)], axis=0)
        else:
            nt = np_scr[p, 0:H, :]
            r1 = jnp.concatenate([jnp.where(jnp.logical_and(hi_h, left_h), nt, 0.0), zero_h,
                                  jnp.where(jnp.logical_and(hi_h, right_h), nt, 0.0), zero_h], axis=0)
        xs.append(_split3_dot(tp, r1))
    ys = []
    for (cc, pr, d), x in zip(packs, xs):
        tp = t_packs[cc * n_per + pr * 2 + d]
        if d == 0:
            r2 = jnp.concatenate([jnp.where(cih < H, tp, 0.0), zero_h,
                                  jnp.where(jnp.logical_and(cih >= C, cih < C + H), tp, 0.0), zero_h], axis=0)
        else:
            r2 = jnp.concatenate([zero_h, jnp.where(jnp.logical_and(cih >= H, cih < C), tp, 0.0),
                                  zero_h, jnp.where(cih >= C + H, tp, 0.0)], axis=0)
        ys.append(_split3_dot(x, r2))
    for (cc, pr, d), y in zip(packs, ys):
        h0, h1 = 2 * pr, 2 * pr + 1
        s0 = slice(h0 * DN_DK, (h0 + 1) * DN_DK)
        s1 = slice(h1 * DN_DK, (h1 + 1) * DN_DK)
        rows = rows_of[cc]
        tp = t_packs[cc * n_per + pr * 2 + d]
        if d == 0:
            t_top = jnp.where(lo_h, tp, 0.0)
            t_bot = jnp.where(lo_h, -y, tp)
        else:
            t_top = jnp.where(lo_h, tp, -y)
            t_bot = jnp.where(lo_h, 0.0, tp)
        t_p = jnp.concatenate([t_top, t_bot], axis=0)
        rhs_bd = _block_diag2(scaled(cc, h0, d), scaled(cc, h1, d)).astype(bf16)
        t_hi = t_p.astype(bf16)
        t_lo = (t_p - t_hi.astype(f32)).astype(bf16)
        sol = jnp.dot(jnp.concatenate([t_hi, t_lo], axis=1), jnp.concatenate([rhs_bd, rhs_bd], axis=0),
                      preferred_element_type=f32)
        uc_ref[d, 0, rows, s0] = sol[:, 0:DN_DV]
        wq_ref[d, 0, cc, 0:C, s0] = sol[:, DN_DV:2 * DN_DV].astype(bf16)
        uc_ref[d, 0, rows, s1] = sol[:, 2 * DN_DV:3 * DN_DV]
        wq_ref[d, 0, cc, 0:C, s1] = sol[:, 3 * DN_DV:4 * DN_DV].astype(bf16)


def _dn_intra(q, k, v, gb, gbt):
    B, Lt, W = q.shape
    C = DN_CHUNK
    nc = Lt // C
    ch = DN_INTRA_CHUNKS
    tok = lambda w: pl.BlockSpec((1, ch * C, w), lambda b, i: (b, i, 0))
    return pl.pallas_call(
        _dn_intra_body,
        out_shape=(jax.ShapeDtypeStruct((2, B, Lt, W), f32),
                   jax.ShapeDtypeStruct((2, B, nc, 2 * C, W), bf16),
                   jax.ShapeDtypeStruct((2, B, nc, DN_PAIRS, C + DN_DK, 2 * C), bf16)),
        grid=(B, nc // ch),
        in_specs=[tok(W), tok(W), tok(W), tok(LANES),
                  pl.BlockSpec((1, ch, 16, C), lambda b, i: (b, i, 0, 0))],
        out_specs=(pl.BlockSpec((2, 1, ch * C, W), lambda b, i: (0, b, i, 0)),
                   pl.BlockSpec((2, 1, ch, 2 * C, W), lambda b, i: (0, b, i, 0, 0)),
                   pl.BlockSpec((2, 1, ch, DN_PAIRS, C + DN_DK, 2 * C), lambda b, i: (0, b, i, 0, 0, 0))),
        scratch_shapes=[pltpu.VMEM((ch * 2 * DN_PAIRS, C, 2 * C), f32),
                        pltpu.VMEM((ch * 2 * DN_PAIRS, DN_HALF, 2 * C), f32)],
        compiler_params=_cparams(("arbitrary", "arbitrary")),
        name="dn_intra",
    )(q, k, v, gb, gbt)


def _dn_state_body(*refs, rev):
    if rev:
        uc_ref, wq_ref, qkkd_ref, gb_ref, of_ref, o_ref, s_scr = refs
    else:
        uc_ref, wq_ref, qkkd_ref, gb_ref, o_ref, s_scr = refs
    C = DN_CHUNK
    d = 1 if rev else 0
    last = 0 if rev else C - 1

    @pl.when(pl.program_id(1) == 0)
    def _():
        s_scr[...] = jnp.zeros_like(s_scr)

    for bb in range(DN_STATE_BATCH):
        for pr in range(DN_PAIRS):
            heads = (2 * pr, 2 * pr + 1)
            s_old, o_part, vbs = [], [], []
            for hh in heads:
                hs = slice(hh * DN_DK, (hh + 1) * DN_DK)
                s = s_scr[bb, hh]
                r = jnp.dot(wq_ref[0, bb, 0, :, hs], s.astype(bf16), preferred_element_type=f32)
                vbs.append((uc_ref[0, bb, :, hs] - r[0:C]).astype(bf16))
                o_part.append(r[C:2 * C])
                s_old.append(s)
            r2 = jnp.dot(qkkd_ref[0, bb, 0, pr], _block_diag2(vbs[0], vbs[1]), preferred_element_type=f32)
            for n, hh in enumerate(heads):
                hs = slice(hh * DN_DK, (hh + 1) * DN_DK)
                vs = slice(n * DN_DV, (n + 1) * DN_DV)
                gl = 2 * DN_HEADS + d * DN_HEADS + hh
                eg = jnp.exp(gb_ref[bb, last:last + 1, gl:gl + 1])
                s_scr[bb, hh] = s_old[n] * eg + r2[C:C + DN_DK, vs]
                o_h = o_part[n] + r2[0:C, vs]
                if rev:
                    o_h = o_h + of_ref[bb, :, hs]
                o_ref[bb, :, hs] = o_h


def _dn_state(uc, wq, qkkd, gb, n_ctx_chunks, rev, o_fwd=None):
    _, B, Lt, W = uc.shape
    C = DN_CHUNK
    nc = Lt // C
    nb = DN_STATE_BATCH
    d = 1 if rev else 0
    if rev:
        cidx = lambda i: jnp.where(i < n_ctx_chunks, n_ctx_chunks - 1 - i, nc - 1 - (i - n_ctx_chunks))
    else:
        cidx = lambda i: i
    tok = lambda w: pl.BlockSpec((nb, C, w), lambda b, i: (b, cidx(i), 0))
    in_specs = [pl.BlockSpec((1, nb, C, W), lambda b, i: (d, b, cidx(i), 0)),
                pl.BlockSpec((1, nb, 1, 2 * C, W), lambda b, i: (d, b, cidx(i), 0, 0)),
                pl.BlockSpec((1, nb, 1, DN_PAIRS, C + DN_DK, 2 * C), lambda b, i: (d, b, cidx(i), 0, 0, 0)),
                tok(LANES)]
    args = [uc, wq, qkkd, gb]
    if rev:
        in_specs.append(tok(W))
        args.append(o_fwd)
    return pl.pallas_call(
        functools.partial(_dn_state_body, rev=rev),
        out_shape=jax.ShapeDtypeStruct((B, Lt, W), f32),
        grid=(B // nb, nc),
        in_specs=in_specs,
        out_specs=tok(W),
        scratch_shapes=[pltpu.VMEM((nb, DN_HEADS, DN_DK, DN_DV), f32)],
        compiler_params=_cparams(("arbitrary", "arbitrary")),
        name="dn_state_bwd" if rev else "dn_state_fwd",
    )(*args)


def _attn_body(q_ref, kp_ref, kc_ref, kn_ref, kx_ref, vp_ref, vc_ref, vn_ref, vx_ref, sink_ref, o_ref,
               *, n_ctx_blocks, seq_len):
    Tq = AT_BLOCK
    G = AT_HEADS // AT_KV
    i = pl.program_id(1)
    is_ctx = i < n_ctx_blocks
    nloc = 3 * Tq
    qpos = (i - n_ctx_blocks) * Tq + lax.broadcasted_iota(jnp.int32, (Tq, nloc), 0)
    kpos = (i - n_ctx_blocks - 1) * Tq + lax.broadcasted_iota(jnp.int32, (Tq, nloc), 1)
    ok = jnp.logical_and(jnp.abs(qpos - kpos) <= WINDOW, jnp.logical_and(kpos >= 0, kpos < seq_len))
    ok = jnp.logical_and(ok, jnp.logical_not(is_ctx))
    k_all = jnp.concatenate([kp_ref[0], kc_ref[0], kn_ref[0], kx_ref[0]], axis=0).astype(bf16)
    v_all = jnp.concatenate([vp_ref[0], vc_ref[0], vn_ref[0], vx_ref[0]], axis=0)
    lane_q = lax.broadcasted_iota(jnp.int32, (Tq, LANES), 1)
    lane_v = lax.broadcasted_iota(jnp.int32, v_all.shape, 1)
    ug = AT_UNIT_GROUPS
    units = [(hk, gp) for hk in range(AT_KV) for gp in range(G // ug)]
    nt = (((1,), (1,)), ((), ()))
    ok_u = jnp.concatenate([ok] * ug, axis=0)
    lane_u = jnp.concatenate([lane_q] * ug, axis=0)
    rhs = []
    for hk in range(AT_KV):
        mine = lane_v // AT_HD == hk
        rhs.append(jnp.concatenate([jnp.where(mine, v_all, 0.0), mine.astype(f32)], axis=1).astype(bf16))
    scores, sinks = [], []
    for hk, gp in units:
        gs = range(gp * ug, (gp + 1) * ug)
        qs = jnp.concatenate([jnp.where(lane_q // AT_HD == hk, q_ref[0, :, g * LANES:(g + 1) * LANES], 0.0)
                              for g in gs], axis=0).astype(bf16)
        s = lax.dot_general(qs, k_all, nt, preferred_element_type=f32)
        scores.append(jnp.concatenate([jnp.where(ok_u, s[:, :nloc], -jnp.inf), s[:, nloc:]], axis=1))
        sinks.append(jnp.concatenate([jnp.broadcast_to(sink_ref[0:1, hk * G + g:hk * G + g + 1], (Tq, 1))
                                      for g in gs], axis=0))
    maxes = [jnp.maximum(jnp.max(s, axis=-1, keepdims=True), sk) for s, sk in zip(scores, sinks)]
    probs = [jnp.exp(s - m).astype(bf16) for s, m in zip(scores, maxes)]
    outs = [jnp.dot(p, rhs[hk], preferred_element_type=f32) for p, (hk, gp) in zip(probs, units)]
    for gp in range(G // ug):
        num = jnp.zeros((ug * Tq, LANES), f32)
        den = jnp.zeros((ug * Tq, LANES), f32)
        for u, (hk, gg) in enumerate(units):
            if gg == gp:
                num = num + outs[u][:, 0:LANES]
                den = den + outs[u][:, LANES:2 * LANES] + jnp.where(lane_u // AT_HD == hk,
                                                                    jnp.exp(sinks[u] - maxes[u]), 0.0)
        o = num / den
        for n in range(ug):
            g = gp * ug + n
            o_ref[0, :, g * LANES:(g + 1) * LANES] = o[n * Tq:(n + 1) * Tq, :]


def _attention(aq, ak, av, sink_vec, n_ctx, seq_len):
    B, Lt, _ = aq.shape
    nb = Lt // AT_BLOCK
    n_ctx_blocks = n_ctx // AT_BLOCK
    blk = lambda w, f: pl.BlockSpec((1, AT_BLOCK, w), lambda b, i: (b, f(i), 0))
    lo, hi = n_ctx_blocks, nb - 1
    prev = lambda i: jnp.clip(i - 1, lo, hi)
    cur = lambda i: i
    nxt = lambda i: jnp.clip(i + 1, lo, hi)
    ctx_spec = pl.BlockSpec((1, n_ctx, AT_KW), lambda b, i: (b, 0, 0))
    body = functools.partial(_attn_body, n_ctx_blocks=n_ctx_blocks, seq_len=seq_len)
    return pl.pallas_call(
        body,
        out_shape=jax.ShapeDtypeStruct((B, Lt, AT_QW), f32),
        grid=(B, nb),
        in_specs=[blk(AT_QW, cur),
                  blk(AT_KW, prev), blk(AT_KW, cur), blk(AT_KW, nxt), ctx_spec,
                  blk(AT_KW, prev), blk(AT_KW, cur), blk(AT_KW, nxt), ctx_spec,
                  _const_spec((1, LANES))],
        out_specs=blk(AT_QW, cur),
        compiler_params=_cparams(("arbitrary", "arbitrary")),
        name="window_attn",
    )(aq, ak, ak, ak, ak, av, av, av, av, sink_vec)


def _merge_body(x_ref, mod_ref, g_ref, ys_ref, od_ref, z_ref, yc_ref,
                wg_ref, wglu_ref, ng_ref, wa_ref, wb_ref, wc_ref, wo_ref, o_ref):
    x = x_ref[0]
    h = _modnorm(x, g_ref[...], mod_ref[0, 1:2, :], mod_ref[0, 0:1, :]).astype(bf16)
    zs = jax.nn.gelu(ys_ref[...])
    ya = zs * jax.nn.sigmoid(_dot(zs, wglu_ref[...]))
    ng = ng_ref[...]
    parts = []
    for hh in range(DN_HEADS):
        hs = slice(hh * DN_DV, (hh + 1) * DN_DV)
        o = od_ref[0, :, hs]
        on = o * lax.rsqrt(jnp.mean(o * o, axis=-1, keepdims=True) + EPS) * ng
        parts.append(on * _silu(z_ref[0, :, hs]))
    yb = jnp.concatenate(parts, axis=1)
    yc = yc_ref[0]
    D = D_MODEL

    def gate(j):
        return jax.nn.sigmoid(jnp.dot(h, wg_ref[:, j * D:(j + 1) * D], preferred_element_type=f32))

    m = gate(0) * _dot(ya, wa_ref[...])
    m = m + gate(1) * _dot(yb, wb_ref[...])
    m = m + gate(2) * _dot(yc, wc_ref[...])
    mix = _dot(m, wo_ref[...])
    o_ref[0] = x + mod_ref[0, 2:3, :] * mix


def _merge(xcat, mods, g1, ys_tm, o_dn, dz, yc, w_gate, w_glu, dn_g, w_a, w_b, w_c, w_o, n_ctx_tiles):
    B, Lt, D = xcat.shape
    nt = Lt // TOK_TILE
    tok = lambda w: pl.BlockSpec((1, TOK_TILE, w), lambda b, t: (b, t, 0))
    return pl.pallas_call(
        _merge_body,
        out_shape=jax.ShapeDtypeStruct((B, Lt, D), f32),
        grid=(B, nt),
        in_specs=[tok(D),
                  pl.BlockSpec((1, N_MOD, D), _mod_index(n_ctx_tiles, B)),
                  _const_spec((1, D)),
                  pl.BlockSpec((TOK_TILE, SSM_WIDTH), lambda b, t: (t, b)),
                  tok(DN_W), tok(DN_W), tok(AT_QW),
                  _const_spec(w_gate.shape), _const_spec(w_glu.shape), _const_spec((1, DN_DV)),
                  _const_spec(w_a.shape), _const_spec(w_b.shape), _const_spec(w_c.shape),
                  _const_spec(w_o.shape)],
        out_specs=tok(D),
        compiler_params=_cparams(("arbitrary", "arbitrary")),
        name="merge_out",
    )(xcat, mods, g1, ys_tm, o_dn, dz, yc, w_gate, w_glu, dn_g, w_a, w_b, w_c, w_o)


FF_CHUNK = 1024


def _ffn_body(x_ref, mod_ref, g_ref, w1_ref, w2_ref, fg_ref, o_ref):
    x = x_ref[0]
    h = _modnorm(x, g_ref[...], mod_ref[0, 4:5, :], mod_ref[0, 3:4, :]).astype(bf16)
    acc = jnp.zeros(x.shape, f32)
    for j in range(D_FF // FF_CHUNK):
        a = jnp.dot(h, w1_ref[:, j * FF_CHUNK:(j + 1) * FF_CHUNK], preferred_element_type=f32)
        a = jnp.square(jnp.maximum(a, 0.0)).astype(bf16)
        acc = acc + jnp.dot(a, w2_ref[j * FF_CHUNK:(j + 1) * FF_CHUNK, :], preferred_element_type=f32)
    y = x + mod_ref[0, 5:6, :] * acc
    if fg_ref is not None:
        y = y * lax.rsqrt(jnp.mean(y * y, axis=-1, keepdims=True) + EPS) * fg_ref[...]
    o_ref[0] = y


def _ffn_mid_body(x_ref, mod_ref, g_ref, w1_ref, w2_ref, o_ref):
    _ffn_body(x_ref, mod_ref, g_ref, w1_ref, w2_ref, None, o_ref)


def _ffn(xcat, mods, g2, w1, w2, n_ctx_tiles, final_g=None):
    B, Lt, D = xcat.shape
    nt = Lt // TOK_TILE
    tok = pl.BlockSpec((1, TOK_TILE, D), lambda b, t: (b, t, 0))
    weights = [_const_spec((1, D)), _const_spec(w1.shape), _const_spec(w2.shape)]
    if final_g is None:
        return pl.pallas_call(
            _ffn_mid_body,
            out_shape=jax.ShapeDtypeStruct((B, Lt, D), f32),
            grid=(B, nt),
            in_specs=[tok, pl.BlockSpec((1, N_MOD, D), _mod_index(n_ctx_tiles, B))] + weights,
            out_specs=tok,
            compiler_params=_cparams(("arbitrary", "arbitrary")),
            name="ffn",
        )(xcat, mods, g2, w1, w2)
    return pl.pallas_call(
        _ffn_body,
        out_shape=jax.ShapeDtypeStruct((B, Lt - n_ctx_tiles * TOK_TILE, D), f32),
        grid=(B, nt - n_ctx_tiles),
        in_specs=[pl.BlockSpec((1, TOK_TILE, D), lambda b, t: (b, t + n_ctx_tiles, 0)),
                  pl.BlockSpec((1, N_MOD, D), lambda b, t: (b, 0, 0))] + weights + [_const_spec((1, D))],
        out_specs=tok,
        compiler_params=_cparams(("arbitrary", "arbitrary")),
        name="ffn_final",
    )(xcat, mods, g2, w1, w2, final_g)


def _rope_tables(n_ctx, seq_len):
    n = AT_HD // 4
    inv_freq = ROPE_BASE ** (-jnp.arange(n, dtype=f32) / n)
    pos = jnp.arange(seq_len, dtype=jnp.int32)
    rows = (pos // GRID_W).astype(f32)[:, None] * inv_freq[None, :]
    cols = (pos % GRID_W).astype(f32)[:, None] * inv_freq[None, :]
    cos = jnp.concatenate([jnp.cos(rows)] * 2 + [jnp.cos(cols)] * 2, axis=1)
    sin = jnp.concatenate([-jnp.sin(rows), jnp.sin(rows), -jnp.sin(cols), jnp.sin(cols)], axis=1)
    cos = jnp.concatenate([jnp.ones((n_ctx, AT_HD), f32), cos], axis=0)
    sin = jnp.concatenate([jnp.zeros((n_ctx, AT_HD), f32), sin], axis=0)
    return jnp.tile(cos, (1, LANES // AT_HD)), jnp.tile(sin, (1, LANES // AT_HD))


def _attn_head_perm():
    G = AT_HEADS // AT_KV
    n = jnp.arange(AT_QW)
    g, hk, e = n // LANES, (n % LANES) // AT_HD, n % AT_HD
    return (hk * G + g) * AT_HD + e


def _split_w_in(w_in):
    o = SSM_WIDTH + 4 * DN_W
    w_ba = w_in[:, o:o + 4 * DN_HEADS]
    o2 = o + 4 * DN_HEADS
    w_att = w_in[:, o2:o2 + AT_QW + 2 * AT_KW]
    w_att = jnp.concatenate([w_att[:, :AT_QW][:, _attn_head_perm()], w_att[:, AT_QW:]], axis=1)
    w_gate = w_in[:, o2 + AT_QW + 2 * AT_KW:]
    w_main = jnp.concatenate([w_in[:, :o], w_att], axis=1).astype(bf16)
    w_ba = jnp.pad(w_ba, ((0, 0), (0, LANES - 4 * DN_HEADS))).astype(bf16)
    return w_main, w_ba, w_gate.astype(bf16)


def kernel(x, c, ctx, c_ctx, norm1_g, norm2_g, w_mod, b_mod, w_in, ssm_lam_re, ssm_lam_im, ssm_log_dt, ssm_b_re, ssm_b_im, ssm_c_re, ssm_c_im, ssm_d, ssm_w_glu, dn_conv_w, dn_a_log, dn_dt_bias, dn_norm_g, attn_sink, w_branch_a, w_branch_b, w_branch_c, w_out, w_ff1, w_ff2, final_norm_g):
    B, L, D = x.shape
    Lc = ctx.shape[1]
    depth = w_in.shape[0]
    assert B == SUBLANES and D == D_MODEL
    assert Lc % TOK_TILE == 0 and L % TOK_TILE == 0
    Lt = Lc + L
    n_ctx_tiles = Lc // TOK_TILE

    cond = jnp.concatenate([c, c_ctx[None, :], jnp.zeros((16 - B - 1, D), f32)], axis=0)
    mods_all = _modulation(cond, w_mod, b_mod).reshape(depth, 16, N_MOD, D)
    cos_t, sin_t = _rope_tables(Lc, L)
    xcat = jnp.concatenate([ctx, x], axis=1)

    for layer in range(depth):
        mods = mods_all[layer]
        g1 = norm1_g[layer].reshape(1, D)
        g2 = norm2_g[layer].reshape(1, D)
        w_main, w_ba, w_gate = _split_w_in(w_in[layer])

        conv_w = jnp.pad(dn_conv_w[layer], ((0, SUBLANES - DN_CONV), (0, 0)))
        pad_l = 2 * DN_HEADS
        al_vec = jnp.pad(dn_a_log[layer].reshape(1, 2 * DN_HEADS), ((0, 0), (pad_l, LANES - 2 * pad_l)))
        dtb_vec = jnp.pad(dn_dt_bias[layer].reshape(1, 2 * DN_HEADS), ((0, 0), (pad_l, LANES - 2 * pad_l)))
        u_tm, dz, aq, ak, av, qn, kn, vs, gb, gbt = _inproj(xcat, mods, g1, w_main, w_ba, cos_t, sin_t,
                                                            conv_w, al_vec, dtb_vec, n_ctx_tiles)

        a_re, a_im, bbt_re, bbt_im = _s5_discretize(ssm_lam_re[layer], ssm_lam_im[layer], ssm_log_dt[layer],
                                                    ssm_b_re[layer], ssm_b_im[layer])
        bmat, cmat, avec = _s5_matrices(a_re, a_im, bbt_re, bbt_im, ssm_c_re[layer], ssm_c_im[layer])
        u_rows = u_tm.reshape(Lt * B, SSM_WIDTH)
        y_f = _s5_scan(u_rows, bmat, cmat, avec, Lc, rev=False)
        y_s5 = _s5_scan(u_rows, bmat, cmat, avec, Lc, rev=True, yf=y_f,
                        d_skip=ssm_d[layer].reshape(1, SSM_WIDTH))
        ys_tm = y_s5.reshape(Lt, B * SSM_WIDTH)

        n_ctx_chunks = Lc // DN_CHUNK
        uc, wq, qkkd = _dn_intra(qn, kn, vs, gb, gbt)
        o_f = _dn_state(uc, wq, qkkd, gb, n_ctx_chunks, rev=False)
        o_dn = _dn_state(uc, wq, qkkd, gb, n_ctx_chunks, rev=True, o_fwd=o_f)

        sink_vec = jnp.pad(attn_sink[layer].reshape(1, AT_HEADS), ((0, 0), (0, LANES - AT_HEADS)))
        yc = _attention(aq, ak, av, sink_vec, Lc, L)

        x1 = _merge(xcat, mods, g1, ys_tm, o_dn, dz, yc, w_gate, ssm_w_glu[layer].astype(bf16),
                    dn_norm_g[layer].reshape(1, DN_DV), w_branch_a[layer].astype(bf16),
                    w_branch_b[layer].astype(bf16), w_branch_c[layer][_attn_head_perm(), :].astype(bf16),
                    w_out[layer].astype(bf16), n_ctx_tiles)
        final_g = final_norm_g.reshape(1, D) if layer == depth - 1 else None
        xcat = _ffn(x1, mods, g2, w_ff1[layer].astype(bf16), w_ff2[layer].astype(bf16), n_ctx_tiles, final_g)

    return xcat
```
